```python
import jax, jax.numpy as jnp
from jax import lax
import numpy as np

D_MODEL = 2048
BATCH = 2
SEQ = 16384
DEPTH = 1

CHUNK = 64
QUERY_BLOCK = 128
SB_HEADS = 8
SB_HEAD_DIM = 128
MLA_HEADS = 8
MLA_NOPE_DIM = 128
MLA_ROPE_DIM = 64
MLA_V_DIM = 128
MLA_Q_RANK = 768
MLA_KV_RANK = 512
N_BRANCHES = 2
D_FF = -(-8 * D_MODEL // (3 * 256)) * 256
ROPE_THETA = 10000.0
NORM_EPS = 1e-6
NEG_INF = -1e30

SB_WIDTH = SB_HEADS * SB_HEAD_DIM
MLA_QK_DIM = MLA_NOPE_DIM + MLA_ROPE_DIM
IN_WIDTHS = [SB_WIDTH, SB_WIDTH, SB_WIDTH, MLA_Q_RANK, MLA_KV_RANK, MLA_ROPE_DIM, N_BRANCHES * D_MODEL]
D_IN = int(sum(IN_WIDTHS))
IN_OFFSETS = [int(o) for o in np.cumsum(IN_WIDTHS)[:-1]]

kernel_name = "sandwich_gated_stickbreaking_mla_swiglu"


def rms_norm(x, g):
    xf = x.astype(jnp.float32)
    y = xf * lax.rsqrt(jnp.mean(xf * xf, axis=-1, keepdims=True) + NORM_EPS)
    return (y * g.astype(jnp.float32)).astype(x.dtype)


def rope(x, positions):
    half = MLA_ROPE_DIM // 2
    inv_freq = ROPE_THETA ** (-jnp.arange(half, dtype=jnp.float32) / half)
    ang = positions.astype(jnp.float32)[..., None] * inv_freq
    if x.ndim == 4:
        ang = ang[:, :, None, :]
    cos, sin = jnp.cos(ang), jnp.sin(ang)
    xf = x.astype(jnp.float32)
    x1, x2 = xf[..., :half], xf[..., half:]
    return jnp.concatenate([x1 * cos - x2 * sin, x2 * cos + x1 * sin], axis=-1).astype(x.dtype)


def to_blocks(t):
    b, s = t.shape[:2]
    return jnp.moveaxis(t.reshape(b, s // QUERY_BLOCK, QUERY_BLOCK, *t.shape[2:]), 1, 0)


def from_blocks(t):
    t = jnp.moveaxis(t, 0, 1)
    return t.reshape(t.shape[0], -1, *t.shape[3:])


def stick_breaking_attention(q, k, v):
    seq = k.shape[1]
    scale = SB_HEAD_DIM ** -0.5
    key_pos = jnp.arange(seq)

    def one_block(args):
        qb, blk = args
        z = jnp.einsum('bqhd,bkhd->bhqk', qb, k).astype(jnp.float32) * scale
        q_pos = blk * QUERY_BLOCK + jnp.arange(QUERY_BLOCK)
        valid = key_pos[None, :] < q_pos[:, None]
        log_keep = jnp.where(valid, jax.nn.log_sigmoid(-z), 0.0)
        log_suffix = lax.cumsum(log_keep, axis=3, reverse=True) - log_keep
        w = jnp.where(valid, jnp.exp(jax.nn.log_sigmoid(z) + log_suffix), 0.0)
        return jnp.einsum('bhqk,bkhd->bqhd', w.astype(v.dtype), v)

    n_blocks = q.shape[1] // QUERY_BLOCK
    return from_blocks(lax.map(one_block, (to_blocks(q), jnp.arange(n_blocks))))


def mla_attention(q_nope, q_rope, k_nope, k_rope, v):
    seq = k_nope.shape[1]
    scale = MLA_QK_DIM ** -0.5
    key_chunk = jnp.arange(seq) // CHUNK

    def one_block(args):
        qn, qr, blk = args
        s = (jnp.einsum('bqhd,bkhd->bhqk', qn, k_nope)
             + jnp.einsum('bqhr,bkr->bhqk', qr, k_rope)).astype(jnp.float32) * scale
        q_chunk = (blk * QUERY_BLOCK + jnp.arange(QUERY_BLOCK)) // CHUNK
        allowed = key_chunk[None, :] <= q_chunk[:, None]
        p = jax.nn.softmax(jnp.where(allowed, s, NEG_INF), axis=-1)
        return jnp.einsum('bhqk,bkhd->bqhd', p.astype(v.dtype), v)

    n_blocks = q_nope.shape[1] // QUERY_BLOCK
    out = lax.map(one_block, (to_blocks(q_nope), to_blocks(q_rope), jnp.arange(n_blocks)))
    return from_blocks(out)


def setup_inputs(seed: int = 0) -> dict:
    key = jax.random.key(seed)
    ks = jax.random.split(key, 20)
    f32 = jnp.float32

    def w(k, shape, fan_in):
        return jax.random.normal(k, shape, f32) * fan_in ** -0.5

    def gain(k, dim):
        return 1.0 + 0.1 * jax.random.normal(k, (DEPTH, dim), f32)

    x = jax.random.normal(ks[0], (BATCH, SEQ, D_MODEL), f32)
    offsets = jax.random.randint(ks[1], (BATCH, 1), 0, 4096) * CHUNK
    positions = (offsets + jnp.arange(SEQ, dtype=jnp.int32)[None, :]).astype(jnp.int32)
    return {
        "x": x,
        "positions": positions,
        "norm_pre_mix": gain(ks[2], D_MODEL),
        "w_in": w(ks[3], (DEPTH, D_MODEL, D_IN), D_MODEL),
        "b_gate": 0.1 * jax.random.normal(ks[4], (DEPTH, N_BRANCHES * D_MODEL), f32),
        "q_norm": gain(ks[5], MLA_Q_RANK),
        "w_uq": w(ks[6], (DEPTH, MLA_Q_RANK, MLA_HEADS * MLA_QK_DIM), MLA_Q_RANK),
        "kv_norm": gain(ks[7], MLA_KV_RANK),
        "w_ukv": w(ks[8], (DEPTH, MLA_KV_RANK, MLA_HEADS * (MLA_NOPE_DIM + MLA_V_DIM)), MLA_KV_RANK),
        "w_o_sb": w(ks[9], (DEPTH, SB_WIDTH, D_MODEL), SB_WIDTH),
        "w_o_mla": w(ks[10], (DEPTH, MLA_HEADS * MLA_V_DIM, D_MODEL), MLA_HEADS * MLA_V_DIM),
        "w_out": w(ks[11], (DEPTH, D_MODEL, D_MODEL), D_MODEL),
        "norm_post_mix": gain(ks[12], D_MODEL),
        "norm_pre_ffn": gain(ks[13], D_MODEL),
        "w_gate_up": w(ks[14], (DEPTH, D_MODEL, 2 * D_FF), D_MODEL),
        "w_down": w(ks[15], (DEPTH, D_FF, D_MODEL), D_FF),
        "norm_post_ffn": gain(ks[16], D_MODEL),
    }


def reference(x, positions, norm_pre_mix, w_in, b_gate, q_norm, w_uq, kv_norm, w_ukv,
              w_o_sb, w_o_mla, w_out, norm_post_mix, norm_pre_ffn, w_gate_up, w_down,
              norm_post_ffn):
    b, s, _ = x.shape
    h = x
    for l in range(DEPTH):
        u = rms_norm(h, norm_pre_mix[l])
        proj = u @ w_in[l]
        q_sb, k_sb, v_sb, q_lat, kv_lat, k_r, gate_logits = jnp.split(proj, IN_OFFSETS, axis=-1)

        q_sb = q_sb.reshape(b, s, SB_HEADS, SB_HEAD_DIM)
        k_sb = k_sb.reshape(b, s, SB_HEADS, SB_HEAD_DIM)
        v_sb = v_sb.reshape(b, s, SB_HEADS, SB_HEAD_DIM)
        y_sb = stick_breaking_attention(q_sb, k_sb, v_sb).reshape(b, s, SB_WIDTH) @ w_o_sb[l]

        q = (rms_norm(q_lat, q_norm[l]) @ w_uq[l]).reshape(b, s, MLA_HEADS, MLA_QK_DIM)
        q_nope, q_rope = q[..., :MLA_NOPE_DIM], rope(q[..., MLA_NOPE_DIM:], positions)
        kv = (rms_norm(kv_lat, kv_norm[l]) @ w_ukv[l]).reshape(b, s, MLA_HEADS, MLA_NOPE_DIM + MLA_V_DIM)
        k_nope, v_mla = kv[..., :MLA_NOPE_DIM], kv[..., MLA_NOPE_DIM:]
        k_rope = rope(k_r, positions)
        y_mla = mla_attention(q_nope, q_rope, k_nope, k_rope, v_mla).reshape(b, s, MLA_HEADS * MLA_V_DIM) @ w_o_mla[l]

        gates = jax.nn.sigmoid((gate_logits + b_gate[l]).astype(jnp.float32)).astype(h.dtype)
        g_sb, g_mla = gates[..., :D_MODEL], gates[..., D_MODEL:]
        mixed = (g_sb * y_sb + g_mla * y_mla) @ w_out[l]
        h = h + rms_norm(mixed, norm_post_mix[l])

        gu = rms_norm(h, norm_pre_ffn[l]) @ w_gate_up[l]
        g, up = gu[..., :D_FF], gu[..., D_FF:]
        h = h + rms_norm((jax.nn.silu(g) * up) @ w_down[l], norm_post_ffn[l])
    return h
```

```python
import functools

import jax
import jax.numpy as jnp
import numpy as np
from jax import lax
from jax.experimental import pallas as pl
from jax.experimental.pallas import tpu as pltpu

F32 = jnp.float32
BF16 = jnp.bfloat16

CHUNK = 64
SB_HEADS = 8
SB_HEAD_DIM = 128
MLA_HEADS = 8
MLA_NOPE_DIM = 128
MLA_ROPE_DIM = 64
MLA_V_DIM = 128
MLA_Q_RANK = 768
MLA_KV_RANK = 512
ROPE_THETA = 10000.0
NORM_EPS = 1e-6
NEG_INF = -1e30

LANES = 128
MLA_QK_PAD = 2 * LANES
VMEM_LIMIT = 56 * 1024 * 1024


def _params(*semantics):
    return pltpu.CompilerParams(dimension_semantics=semantics, vmem_limit_bytes=VMEM_LIMIT)


def _rms(x, g):
    ms = jnp.mean(x * x, axis=-1, keepdims=True)
    return x * lax.rsqrt(ms + NORM_EPS) * g


def _norm_matmul_kernel(x_ref, g_ref, w_ref, cs_ref, o_ref, u_ref):
    @pl.when(pl.program_id(1) == 0)
    def _():
        u_ref[...] = _rms(x_ref[...], g_ref[...]).astype(BF16)

    acc = jnp.dot(u_ref[...], w_ref[...], preferred_element_type=F32)
    o_ref[...] = (acc * cs_ref[...]).astype(o_ref.dtype)


def norm_matmul(x, gain, w, col_scale, *, tm, tn):
    m, d = x.shape
    n = w.shape[1]
    tm = min(tm, m)
    return pl.pallas_call(
        _norm_matmul_kernel,
        grid=(m // tm, n // tn),
        in_specs=[
            pl.BlockSpec((tm, d), lambda i, j: (i, 0)),
            pl.BlockSpec((1, d), lambda i, j: (0, 0)),
            pl.BlockSpec((d, tn), lambda i, j: (0, j)),
            pl.BlockSpec((1, tn), lambda i, j: (0, j)),
        ],
        out_specs=pl.BlockSpec((tm, tn), lambda i, j: (i, j)),
        out_shape=jax.ShapeDtypeStruct((m, n), BF16),
        scratch_shapes=[pltpu.VMEM((tm, d), BF16)],
        compiler_params=_params("parallel", "arbitrary"),
        name="norm_matmul",
    )(x, gain, w, col_scale)


def _rope128(r, rope_ref):
    c = rope_ref[:, 0:LANES]
    a = rope_ref[:, LANES:2 * LANES]
    b = rope_ref[:, 2 * LANES:3 * LANES]
    half = MLA_ROPE_DIM // 2
    return r * c + pltpu.roll(r, LANES - half, 1) * a + pltpu.roll(r, half, 1) * b


def _mla_prep_kernel(x_ref, g_ref, wlat_ref, qn_ref, wuq_ref, kvn_ref, wukv_ref, rope_ref,
                     q_ref, k_ref, v_ref, *, q_scale):
    u = _rms(x_ref[...], g_ref[...]).astype(BF16)
    lat = jnp.dot(u, wlat_ref[...], preferred_element_type=F32)
    q_lat = lat[:, :MLA_Q_RANK]
    kv_lat = lat[:, MLA_Q_RANK:MLA_Q_RANK + MLA_KV_RANK]
    k_r = lat[:, MLA_Q_RANK + MLA_KV_RANK:]

    q = jnp.dot(_rms(q_lat, qn_ref[...]).astype(BF16), wuq_ref[...], preferred_element_type=F32)
    kv = jnp.dot(_rms(kv_lat, kvn_ref[...]).astype(BF16), wukv_ref[...], preferred_element_type=F32)
    k_rope = _rope128(k_r, rope_ref).astype(BF16)

    kv_split = MLA_HEADS * MLA_NOPE_DIM
    for h in range(MLA_HEADS):
        lo = h * MLA_QK_PAD
        q_nope = q[:, lo:lo + LANES]
        q_rope = _rope128(q[:, lo + LANES:lo + 2 * LANES], rope_ref)
        q_ref[:, lo:lo + LANES] = (q_nope * q_scale).astype(BF16)
        q_ref[:, lo + LANES:lo + 2 * LANES] = (q_rope * q_scale).astype(BF16)
        k_ref[:, lo:lo + LANES] = kv[:, h * LANES:(h + 1) * LANES].astype(BF16)
        k_ref[:, lo + LANES:lo + 2 * LANES] = k_rope
    v_ref[...] = kv[:, kv_split:].astype(BF16)


def mla_prep(x, gain, w_lat, q_norm, w_uq, kv_norm, w_ukv, rope_tab, *, tm, q_scale):
    m, d = x.shape
    tm = min(tm, m)
    const = lambda i: (0, 0)
    qk_w = MLA_HEADS * MLA_QK_PAD
    v_w = MLA_HEADS * MLA_V_DIM
    return pl.pallas_call(
        functools.partial(_mla_prep_kernel, q_scale=q_scale),
        grid=(m // tm,),
        in_specs=[
            pl.BlockSpec((tm, d), lambda i: (i, 0)),
            pl.BlockSpec(gain.shape, const),
            pl.BlockSpec(w_lat.shape, const),
            pl.BlockSpec(q_norm.shape, const),
            pl.BlockSpec(w_uq.shape, const),
            pl.BlockSpec(kv_norm.shape, const),
            pl.BlockSpec(w_ukv.shape, const),
            pl.BlockSpec((tm, 3 * LANES), lambda i: (i, 0)),
        ],
        out_specs=[
            pl.BlockSpec((tm, qk_w), lambda i: (i, 0)),
            pl.BlockSpec((tm, qk_w), lambda i: (i, 0)),
            pl.BlockSpec((tm, v_w), lambda i: (i, 0)),
        ],
        out_shape=[
            jax.ShapeDtypeStruct((m, qk_w), BF16),
            jax.ShapeDtypeStruct((m, qk_w), BF16),
            jax.ShapeDtypeStruct((m, v_w), BF16),
        ],
        compiler_params=_params("parallel"),
        name="mla_prep",
    )(x, gain, w_lat, q_norm, w_uq, kv_norm, w_ukv, rope_tab)


def _qkt(q, k):
    return lax.dot_general(q, k, (((1,), (1,)), ((), ())), preferred_element_type=F32)


def _sb_attn_kernel(q_ref, k_ref, v_ref, o_ref, *, tq, tk):
    i = pl.program_id(2)
    q = q_ref[...]
    sub = tq // tk
    row = lax.broadcasted_iota(jnp.int32, (tk, tk), 0)
    col = lax.broadcasted_iota(jnp.int32, (tk, tk), 1)
    later = (row > col).astype(BF16)

    def block(start, run, acc, mask):
        kb = k_ref[pl.ds(start, tk), :]
        vb = v_ref[pl.ds(start, tk), :]
        z = _qkt(q, kb)
        log_keep = -(jnp.maximum(z, 0.0) + jnp.log(1.0 + jnp.exp(-jnp.abs(z))))
        if mask is not None:
            log_keep = jnp.where(mask, log_keep, 0.0)
        suffix = jnp.dot(log_keep.astype(BF16), later, preferred_element_type=F32) + run
        w = jnp.exp(z + log_keep + suffix)
        if mask is not None:
            w = jnp.where(mask, w, 0.0)
        acc = acc + jnp.dot(w.astype(BF16), vb, preferred_element_type=F32)
        run = run + jnp.sum(log_keep, axis=-1, keepdims=True)
        return run, acc

    run = jnp.zeros((tq, 1), F32)
    acc = jnp.zeros((tq, q.shape[1]), F32)
    q_pos = lax.broadcasted_iota(jnp.int32, (tq, tk), 0)
    k_pos = lax.broadcasted_iota(jnp.int32, (tq, tk), 1)
    for s in reversed(range(sub)):
        start = pl.multiple_of(i * tq + s * tk, tk)
        run, acc = block(start, run, acc, (k_pos + s * tk) < q_pos)

    n_full = i * sub

    def body(t, carry):
        start = pl.multiple_of((n_full - 1 - t) * tk, tk)
        return block(start, carry[0], carry[1], None)

    run, acc = lax.fori_loop(0, n_full, body, (run, acc))
    o_ref[...] = acc.astype(o_ref.dtype)


def sb_attention(qkv, *, batch, seq, tq, tk):
    m = qkv.shape[0]
    d = SB_HEAD_DIM
    tq = min(tq, seq)
    tk = min(tk, tq)
    nq = seq // tq
    return pl.pallas_call(
        functools.partial(_sb_attn_kernel, tq=tq, tk=tk),
        grid=(batch, SB_HEADS, nq),
        in_specs=[
            pl.BlockSpec((tq, d), lambda b, h, i: (b * nq + i, h)),
            pl.BlockSpec((seq, d), lambda b, h, i: (b, SB_HEADS + h)),
            pl.BlockSpec((seq, d), lambda b, h, i: (b, 2 * SB_HEADS + h)),
        ],
        out_specs=pl.BlockSpec((tq, d), lambda b, h, i: (b * nq + i, h)),
        out_shape=jax.ShapeDtypeStruct((m, SB_HEADS * d), BF16),
        compiler_params=_params("parallel", "parallel", "arbitrary"),
        name="sb_attention",
    )(qkv, qkv, qkv)


def _mla_attn_kernel(q_ref, k_ref, v_ref, o_ref, *, t):
    i = pl.program_id(2)
    q = q_ref[...]

    def block(start, m, l, acc, mask):
        kb = k_ref[pl.ds(start, t), :]
        vb = v_ref[pl.ds(start, t), :]
        s = _qkt(q, kb)
        if mask is not None:
            s = jnp.where(mask, s, NEG_INF)
        m_new = jnp.maximum(m, jnp.max(s, axis=-1, keepdims=True))
        alpha = jnp.exp(m - m_new)
        p = jnp.exp(s - m_new)
        l = alpha * l + jnp.sum(p, axis=-1, keepdims=True)
        acc = alpha * acc + jnp.dot(p.astype(BF16), vb, preferred_element_type=F32)
        return m_new, l, acc

    q_chunk = lax.broadcasted_iota(jnp.int32, (t, t), 0) // CHUNK
    k_chunk = lax.broadcasted_iota(jnp.int32, (t, t), 1) // CHUNK
    carry = block(pl.multiple_of(i * t, t),
                  jnp.full((t, 1), NEG_INF, F32), jnp.zeros((t, 1), F32),
                  jnp.zeros((t, v_ref.shape[1]), F32), k_chunk <= q_chunk)

    def body(j, c):
        return block(pl.multiple_of(j * t, t), c[0], c[1], c[2], None)

    m, l, acc = lax.fori_loop(0, i, body, carry)
    o_ref[...] = (acc / l).astype(o_ref.dtype)


def mla_attention(q, k, v, *, batch, seq, t):
    m = q.shape[0]
    t = min(t, seq)
    assert t % CHUNK == 0
    nq = seq // t
    return pl.pallas_call(
        functools.partial(_mla_attn_kernel, t=t),
        grid=(batch, MLA_HEADS, nq),
        in_specs=[
            pl.BlockSpec((t, MLA_QK_PAD), lambda b, h, i: (b * nq + i, h)),
            pl.BlockSpec((seq, MLA_QK_PAD), lambda b, h, i: (b, h)),
            pl.BlockSpec((seq, MLA_V_DIM), lambda b, h, i: (b, h)),
        ],
        out_specs=pl.BlockSpec((t, MLA_V_DIM), lambda b, h, i: (b * nq + i, h)),
        out_shape=jax.ShapeDtypeStruct((m, MLA_HEADS * MLA_V_DIM), BF16),
        compiler_params=_params("parallel", "parallel", "arbitrary"),
        name="mla_attention",
    )(q, k, v)


def _merge_kernel(x_ref, asb_ref, amla_ref, gl_ref, bg_ref, wosb_ref, womla_ref, wout_ref,
                  gpost_ref, o_ref):
    d = x_ref.shape[1]
    y_sb = jnp.dot(asb_ref[...], wosb_ref[...], preferred_element_type=F32)
    y_mla = jnp.dot(amla_ref[...], womla_ref[...], preferred_element_type=F32)
    gates = jax.nn.sigmoid(gl_ref[...].astype(F32) + bg_ref[...])
    mixed_in = gates[:, :d] * y_sb + gates[:, d:] * y_mla
    mixed = jnp.dot(mixed_in.astype(BF16), wout_ref[...], preferred_element_type=F32)
    o_ref[...] = x_ref[...] + _rms(mixed, gpost_ref[...])


def merge(x, a_sb, a_mla, gate_logits, b_gate, w_o_sb, w_o_mla, w_out, g_post, *, tm):
    m, d = x.shape
    tm = min(tm, m)
    const = lambda i: (0, 0)
    resident = dict(pipeline_mode=pl.Buffered(1))
    return pl.pallas_call(
        _merge_kernel,
        grid=(m // tm,),
        in_specs=[
            pl.BlockSpec((tm, d), lambda i: (i, 0)),
            pl.BlockSpec((tm, a_sb.shape[1]), lambda i: (i, 0)),
            pl.BlockSpec((tm, a_mla.shape[1]), lambda i: (i, 0)),
            pl.BlockSpec((tm, 2 * d), lambda i: (i, 0)),
            pl.BlockSpec(b_gate.shape, const),
            pl.BlockSpec(w_o_sb.shape, const, **resident),
            pl.BlockSpec(w_o_mla.shape, const, **resident),
            pl.BlockSpec(w_out.shape, const, **resident),
            pl.BlockSpec(g_post.shape, const),
        ],
        out_specs=pl.BlockSpec((tm, d), lambda i: (i, 0)),
        out_shape=jax.ShapeDtypeStruct((m, d), F32),
        compiler_params=_params("parallel"),
        name="merge",
    )(x, a_sb, a_mla, gate_logits, b_gate, w_o_sb, w_o_mla, w_out, g_post)


def _ffn_kernel(h_ref, gpre_ref, wg_ref, wu_ref, wd_ref, gpost_ref, o_ref, hn_ref, acc_ref):
    f = pl.program_id(1)

    @pl.when(f == 0)
    def _():
        hn_ref[...] = _rms(h_ref[...], gpre_ref[...]).astype(BF16)
        acc_ref[...] = jnp.zeros_like(acc_ref)

    hn = hn_ref[...]
    g = jnp.dot(hn, wg_ref[...], preferred_element_type=F32)
    up = jnp.dot(hn, wu_ref[...], preferred_element_type=F32)
    act = (g * jax.nn.sigmoid(g)) * up
    acc_ref[...] += jnp.dot(act.astype(BF16), wd_ref[...], preferred_element_type=F32)

    @pl.when(f == pl.num_programs(1) - 1)
    def _():
        o_ref[...] = h_ref[...] + _rms(acc_ref[...], gpost_ref[...])


def ffn(h, g_pre, w_gate_up, w_down, g_post, *, tm, tf):
    m, d = h.shape
    d_ff = w_down.shape[0]
    tm = min(tm, m)
    nf = d_ff // tf
    const = lambda i, f: (0, 0)
    return pl.pallas_call(
        _ffn_kernel,
        grid=(m // tm, nf),
        in_specs=[
            pl.BlockSpec((tm, d), lambda i, f: (i, 0)),
            pl.BlockSpec(g_pre.shape, const),
            pl.BlockSpec((d, tf), lambda i, f: (0, f)),
            pl.BlockSpec((d, tf), lambda i, f: (0, nf + f)),
            pl.BlockSpec((tf, d), lambda i, f: (f, 0)),
            pl.BlockSpec(g_post.shape, const),
        ],
        out_specs=pl.BlockSpec((tm, d), lambda i, f: (i, 0)),
        out_shape=jax.ShapeDtypeStruct((m, d), F32),
        scratch_shapes=[pltpu.VMEM((tm, d), BF16), pltpu.VMEM((tm, d), F32)],
        compiler_params=_params("parallel", "arbitrary"),
        name="ffn",
    )(h, g_pre, w_gate_up, w_gate_up, w_down, g_post)


def _rope_table(positions):
    half = MLA_ROPE_DIM // 2
    inv_freq = ROPE_THETA ** (-jnp.arange(half, dtype=F32) / half)
    ang = positions.astype(F32)[:, None] * inv_freq
    cos, sin = jnp.cos(ang), jnp.sin(ang)
    m = positions.shape[0]
    one = jnp.ones((m, LANES - MLA_ROPE_DIM), F32)
    zero = jnp.zeros((m, half), F32)
    zpad = jnp.zeros((m, LANES - MLA_ROPE_DIM), F32)
    return jnp.concatenate([cos, cos, one, -sin, zero, zpad, zero, sin, zpad], axis=1)


def _layer(l, x2, pos, norm_pre_mix, w_in, b_gate, q_norm, w_uq, kv_norm, w_ukv, w_o_sb, w_o_mla,
           w_out, norm_post_mix, norm_pre_ffn, w_gate_up, w_down, norm_post_ffn, *, batch, seq):
    d = x2.shape[1]
    sb_w = SB_HEADS * SB_HEAD_DIM
    o_qlat = 3 * sb_w
    o_kvlat = o_qlat + MLA_Q_RANK
    o_kr = o_kvlat + MLA_KV_RANK
    o_gate = o_kr + MLA_ROPE_DIM
    row = lambda v: v[l][None, :].astype(F32)

    w_in_l = w_in[l]
    w_qkv = w_in_l[:, :o_qlat].astype(BF16)
    w_gate = w_in_l[:, o_gate:].astype(BF16)
    w_lat = jnp.concatenate(
        [w_in_l[:, o_qlat:o_gate], jnp.zeros((d, LANES - MLA_ROPE_DIM), w_in_l.dtype)], axis=1).astype(BF16)

    qk_dim = MLA_NOPE_DIM + MLA_ROPE_DIM
    wq = w_uq[l].reshape(MLA_Q_RANK, MLA_HEADS, qk_dim)
    wq = jnp.pad(wq, ((0, 0), (0, 0), (0, MLA_QK_PAD - qk_dim)))
    wq = wq.reshape(MLA_Q_RANK, MLA_HEADS * MLA_QK_PAD).astype(BF16)
    wkv = w_ukv[l].reshape(MLA_KV_RANK, MLA_HEADS, MLA_NOPE_DIM + MLA_V_DIM)
    wkv = jnp.concatenate([wkv[:, :, :MLA_NOPE_DIM].reshape(MLA_KV_RANK, -1),
                           wkv[:, :, MLA_NOPE_DIM:].reshape(MLA_KV_RANK, -1)], axis=1).astype(BF16)

    g_pre = row(norm_pre_mix)
    sb_scale = SB_HEAD_DIM ** -0.5
    cs_qkv = jnp.concatenate([jnp.full((1, sb_w), sb_scale, F32), jnp.ones((1, 2 * sb_w), F32)], axis=1)
    qkv = norm_matmul(x2, g_pre, w_qkv, cs_qkv, tm=1024, tn=1024)
    gate_logits = norm_matmul(x2, g_pre, w_gate, jnp.ones((1, w_gate.shape[1]), F32), tm=1024, tn=1024)

    q_m, k_m, v_m = mla_prep(x2, g_pre, w_lat, row(q_norm), wq, row(kv_norm), wkv, _rope_table(pos),
                             tm=512, q_scale=qk_dim ** -0.5)

    a_sb = sb_attention(qkv, batch=batch, seq=seq, tq=1024, tk=256)
    a_mla = mla_attention(q_m, k_m, v_m, batch=batch, seq=seq, t=512)

    h1 = merge(x2, a_sb, a_mla, gate_logits, row(b_gate), w_o_sb[l].astype(BF16),
               w_o_mla[l].astype(BF16), w_out[l].astype(BF16), row(norm_post_mix), tm=256)
    return ffn(h1, row(norm_pre_ffn), w_gate_up[l].astype(BF16), w_down[l].astype(BF16),
               row(norm_post_ffn), tm=512, tf=512)


def kernel(x, positions, norm_pre_mix, w_in, b_gate, q_norm, w_uq, kv_norm, w_ukv, w_o_sb, w_o_mla,
           w_out, norm_post_mix, norm_pre_ffn, w_gate_up, w_down, norm_post_ffn):
    batch, seq, d = x.shape
    h = x.reshape(batch * seq, d)
    pos = positions.reshape(batch * seq)
    for l in range(w_in.shape[0]):
        h = _layer(l, h, pos, norm_pre_mix, w_in, b_gate, q_norm, w_uq, kv_norm, w_ukv, w_o_sb,
                   w_o_mla, w_out, norm_post_mix, norm_pre_ffn, w_gate_up, w_down, norm_post_ffn,
                   batch=batch, seq=seq)
    return h.reshape(batch, seq, d)
```

```python
import functools
import math

import jax
import jax.numpy as jnp
from jax import lax
from jax.experimental import pallas as pl
from jax.experimental.pallas import tpu as pltpu

F32 = jnp.float32
BF16 = jnp.bfloat16

CHUNK = 64
SB_HEADS = 8
SB_HEAD_DIM = 128
MLA_HEADS = 8
MLA_NOPE_DIM = 128
MLA_ROPE_DIM = 64
MLA_V_DIM = 128
MLA_Q_RANK = 768
MLA_KV_RANK = 512
ROPE_THETA = 10000.0
NORM_EPS = 1e-6
NEG_INF = -1e30
LOG2E = math.log2(math.e)

LANES = 128
MLA_QK_PAD = 2 * LANES
VMEM_LIMIT = 56 * 1024 * 1024
SB_BLOCK = 256
SB_DEAD_LOG2 = -160.0


def _params(*semantics):
    return pltpu.CompilerParams(dimension_semantics=semantics, vmem_limit_bytes=VMEM_LIMIT)


def _rms(x, g):
    ms = jnp.mean(x * x, axis=-1, keepdims=True)
    return x * lax.rsqrt(ms + NORM_EPS) * g


def _norm_matmul_kernel(x_ref, g_ref, w_ref, cs_ref, o_ref, u_ref):
    @pl.when(pl.program_id(1) == 0)
    def _():
        u_ref[...] = _rms(x_ref[...], g_ref[...]).astype(BF16)

    acc = jnp.dot(u_ref[...], w_ref[...], preferred_element_type=F32)
    o_ref[...] = (acc * cs_ref[...]).astype(o_ref.dtype)


def norm_matmul(x, gain, w, col_scale, *, tm, tn):
    m, d = x.shape
    n = w.shape[1]
    tm = min(tm, m)
    return pl.pallas_call(
        _norm_matmul_kernel,
        grid=(m // tm, n // tn),
        in_specs=[
            pl.BlockSpec((tm, d), lambda i, j: (i, 0)),
            pl.BlockSpec((1, d), lambda i, j: (0, 0)),
            pl.BlockSpec((d, tn), lambda i, j: (0, j)),
            pl.BlockSpec((1, tn), lambda i, j: (0, j)),
        ],
        out_specs=pl.BlockSpec((tm, tn), lambda i, j: (i, j)),
        out_shape=jax.ShapeDtypeStruct((m, n), BF16),
        scratch_shapes=[pltpu.VMEM((tm, d), BF16)],
        compiler_params=_params("parallel", "arbitrary"),
        name="norm_matmul",
    )(x, gain, w, col_scale)


def _rope128(r, rope_ref):
    c = rope_ref[:, 0:LANES]
    a = rope_ref[:, LANES:2 * LANES]
    b = rope_ref[:, 2 * LANES:3 * LANES]
    half = MLA_ROPE_DIM // 2
    return r * c + pltpu.roll(r, LANES - half, 1) * a + pltpu.roll(r, half, 1) * b


def _mla_prep_kernel(x_ref, g_ref, wlat_ref, qn_ref, wuq_ref, kvn_ref, wukv_ref, rope_ref,
                     q_ref, k_ref, v_ref, *, q_scale):
    u = _rms(x_ref[...], g_ref[...]).astype(BF16)
    lat = jnp.dot(u, wlat_ref[...], preferred_element_type=F32)
    q_lat = lat[:, :MLA_Q_RANK]
    kv_lat = lat[:, MLA_Q_RANK:MLA_Q_RANK + MLA_KV_RANK]
    k_r = lat[:, MLA_Q_RANK + MLA_KV_RANK:]

    q = jnp.dot(_rms(q_lat, qn_ref[...]).astype(BF16), wuq_ref[...], preferred_element_type=F32)
    kv = jnp.dot(_rms(kv_lat, kvn_ref[...]).astype(BF16), wukv_ref[...], preferred_element_type=F32)
    k_rope = _rope128(k_r, rope_ref).astype(BF16)

    kv_split = MLA_HEADS * MLA_NOPE_DIM
    for h in range(MLA_HEADS):
        lo = h * MLA_QK_PAD
        q_nope = q[:, lo:lo + LANES]
        q_rope = _rope128(q[:, lo + LANES:lo + 2 * LANES], rope_ref)
        q_ref[:, lo:lo + LANES] = (q_nope * q_scale).astype(BF16)
        q_ref[:, lo + LANES:lo + 2 * LANES] = (q_rope * q_scale).astype(BF16)
        k_ref[:, lo:lo + LANES] = kv[:, h * LANES:(h + 1) * LANES].astype(BF16)
        k_ref[:, lo + LANES:lo + 2 * LANES] = k_rope
    v_ref[...] = kv[:, kv_split:].astype(BF16)


def mla_prep(x, gain, w_lat, q_norm, w_uq, kv_norm, w_ukv, rope_tab, *, tm, q_scale):
    m, d = x.shape
    tm = min(tm, m)
    const = lambda i: (0, 0)
    qk_w = MLA_HEADS * MLA_QK_PAD
    v_w = MLA_HEADS * MLA_V_DIM
    return pl.pallas_call(
        functools.partial(_mla_prep_kernel, q_scale=q_scale),
        grid=(m // tm,),
        in_specs=[
            pl.BlockSpec((tm, d), lambda i: (i, 0)),
            pl.BlockSpec(gain.shape, const),
            pl.BlockSpec(w_lat.shape, const),
            pl.BlockSpec(q_norm.shape, const),
            pl.BlockSpec(w_uq.shape, const),
            pl.BlockSpec(kv_norm.shape, const),
            pl.BlockSpec(w_ukv.shape, const),
            pl.BlockSpec((tm, 3 * LANES), lambda i: (i, 0)),
        ],
        out_specs=[
            pl.BlockSpec((tm, qk_w), lambda i: (i, 0)),
            pl.BlockSpec((tm, qk_w), lambda i: (i, 0)),
            pl.BlockSpec((tm, v_w), lambda i: (i, 0)),
        ],
        out_shape=[
            jax.ShapeDtypeStruct((m, qk_w), BF16),
            jax.ShapeDtypeStruct((m, qk_w), BF16),
            jax.ShapeDtypeStruct((m, v_w), BF16),
        ],
        compiler_params=_params("parallel"),
        name="mla_prep",
    )(x, gain, w_lat, q_norm, w_uq, kv_norm, w_ukv, rope_tab)


def _qkt(q, k):
    return lax.dot_general(q, k, (((1,), (1,)), ((), ())), preferred_element_type=F32)


def _neg_abs(x):
    bits = lax.bitcast_convert_type(x, jnp.uint32) | jnp.uint32(0x80000000)
    return lax.bitcast_convert_type(bits, F32)


def _sb_attn_kernel(q_ref, k_ref, v_ref, o_ref, run_ref, acc_ref, *, tq):
    tb = SB_BLOCK
    i = pl.program_id(2)
    subs = range(tq // tb)
    row = lax.broadcasted_iota(jnp.int32, (tb, tb), 0)
    col = lax.broadcasted_iota(jnp.int32, (tb, tb), 1)
    before = row > col
    later = before.astype(BF16)

    def log2_keep(nz):
        return jnp.minimum(nz, 0.0) - jnp.log(1.0 + jnp.exp2(_neg_abs(nz))) * LOG2E

    def suffix(keep):
        return jnp.dot(keep.astype(BF16), later, preferred_element_type=F32)

    q_first = [i * tq + g * tb for g in subs]
    starts = [pl.multiple_of(jnp.maximum(q_first[g] - tb, 0), tb) for g in subs]
    nz = [_qkt(q_ref[g * tb:(g + 1) * tb, :], k_ref[pl.ds(starts[g], 2 * tb), :]) for g in subs]
    nz_lo = [x[:, :tb] for x in nz]
    nz_hi = [x[:, tb:] for x in nz]
    delta = q_first[0] - starts[0]
    valid_lo = [(col - row) < delta] + [None for _ in subs[1:]]
    valid_hi = [(col - row) < delta - tb] + [before for _ in subs[1:]]
    keep_lo = [log2_keep(x) for x in nz_lo]
    keep_lo[0] = jnp.where(valid_lo[0], keep_lo[0], 0.0)
    keep_hi = [jnp.where(valid_hi[g], log2_keep(nz_hi[g]), 0.0) for g in subs]
    suf_hi = [suffix(x) for x in keep_hi]
    suf_lo = [suffix(x) for x in keep_lo]
    run_lo = [suf_hi[g][:, 0:1] + keep_hi[g][:, 0:1] for g in subs]
    w_hi = [jnp.where(valid_hi[g], jnp.exp2((keep_hi[g] - nz_hi[g]) + suf_hi[g]), 0.0) for g in subs]
    w_lo = [jnp.exp2((keep_lo[g] - nz_lo[g]) + suf_lo[g] + run_lo[g]) for g in subs]
    w_lo[0] = jnp.where(valid_lo[0], w_lo[0], 0.0)
    for g in subs:
        w = jnp.concatenate([w_lo[g], w_hi[g]], axis=1).astype(BF16)
        acc_ref[g * tb:(g + 1) * tb, :] = jnp.dot(w, v_ref[pl.ds(starts[g], 2 * tb), :],
                                                  preferred_element_type=F32)
        run_ref[g * tb:(g + 1) * tb, :] = run_lo[g] + suf_lo[g][:, 0:1] + keep_lo[g][:, 0:1]

    def walk_back(g):
        rows = pl.ds(g * tb, tb)

        def alive(c):
            return jnp.logical_and(c[0] >= 0, c[1] > SB_DEAD_LOG2)

        def body(c):
            start = pl.multiple_of(c[0] * tb, tb)
            nz = _qkt(q_ref[rows, :], k_ref[pl.ds(start, tb), :])
            keep = log2_keep(nz)
            suf = suffix(keep)
            run = run_ref[rows, :]
            w = jnp.exp2((keep - nz) + suf + run)
            acc_ref[rows, :] += jnp.dot(w.astype(BF16), v_ref[pl.ds(start, tb), :],
                                        preferred_element_type=F32)
            run = run + suf[:, 0:1] + keep[:, 0:1]
            run_ref[rows, :] = run
            return c[0] - 1, jnp.max(run)

        lax.while_loop(alive, body, (i * len(subs) + g - 2, jnp.max(run_ref[rows, :])))

    @pl.when(jnp.max(run_ref[...]) > SB_DEAD_LOG2)
    def _():
        for g in subs:
            walk_back(g)

    o_ref[...] = acc_ref[...].astype(o_ref.dtype)


def sb_attention(qkv, *, batch, seq, tq):
    m = qkv.shape[0]
    d = SB_HEAD_DIM
    tq = min(tq, seq)
    assert tq % SB_BLOCK == 0 and seq >= 2 * SB_BLOCK
    nq = seq // tq
    return pl.pallas_call(
        functools.partial(_sb_attn_kernel, tq=tq),
        grid=(batch, SB_HEADS, nq),
        in_specs=[
            pl.BlockSpec((tq, d), lambda b, h, i: (b * nq + i, h)),
            pl.BlockSpec((seq, d), lambda b, h, i: (b, SB_HEADS + h)),
            pl.BlockSpec((seq, d), lambda b, h, i: (b, 2 * SB_HEADS + h)),
        ],
        out_specs=pl.BlockSpec((tq, d), lambda b, h, i: (b * nq + i, h)),
        out_shape=jax.ShapeDtypeStruct((m, SB_HEADS * d), BF16),
        scratch_shapes=[pltpu.VMEM((tq, 1), F32), pltpu.VMEM((tq, d), F32)],
        compiler_params=_params("parallel", "parallel", "arbitrary"),
        name="sb_attention",
    )(qkv, qkv, qkv)


def _mla_attn_kernel(q_ref, k_ref, v_ref, o_ref, *, t):
    i = pl.program_id(2)
    q = q_ref[...]
    tq = tk = t

    def update(keys, carry, mask):
        m_old, l, acc = carry
        s = _qkt(q, k_ref[keys, :])
        if mask is not None:
            s = jnp.where(mask, s, NEG_INF)
        m_new = jnp.maximum(m_old, jnp.max(s, axis=-1, keepdims=True))
        alpha = jnp.exp2(m_old - m_new)
        p = jnp.exp2(s - m_new)
        l = alpha * l + jnp.sum(p, axis=-1, keepdims=True)
        pv = jnp.dot(p.astype(BF16), v_ref[keys, :], preferred_element_type=F32)
        return m_new, l, alpha * acc + pv

    q_chunk = lax.broadcasted_iota(jnp.int32, (tq, tk), 0) // CHUNK
    k_chunk = lax.broadcasted_iota(jnp.int32, (tq, tk), 1) // CHUNK
    carry = (jnp.full((tq, 1), NEG_INF, F32), jnp.zeros((tq, 1), F32), jnp.zeros((tq, MLA_V_DIM), F32))
    carry = update(pl.ds(pl.multiple_of(i * tq, tq), tk), carry, k_chunk <= q_chunk)

    def body(j, carry):
        return update(pl.ds(pl.multiple_of(j * tk, tk), tk), carry, None)

    m, l, acc = lax.fori_loop(0, i, body, carry)
    o_ref[...] = (acc / l).astype(o_ref.dtype)


def mla_attention(q, k, v, *, batch, seq, t):
    m = q.shape[0]
    tq = t = min(t, seq)
    assert t % CHUNK == 0
    nq = seq // tq
    return pl.pallas_call(
        functools.partial(_mla_attn_kernel, t=t),
        grid=(batch, MLA_HEADS, nq),
        in_specs=[
            pl.BlockSpec((tq, MLA_QK_PAD), lambda b, h, i: (b * nq + i, h)),
            pl.BlockSpec((seq, MLA_QK_PAD), lambda b, h, i: (b, h)),
            pl.BlockSpec((seq, MLA_V_DIM), lambda b, h, i: (b, h)),
        ],
        out_specs=pl.BlockSpec((tq, MLA_V_DIM), lambda b, h, i: (b * nq + i, h)),
        out_shape=jax.ShapeDtypeStruct((m, MLA_HEADS * MLA_V_DIM), BF16),
        compiler_params=_params("parallel", "parallel", "arbitrary"),
        name="mla_attention",
    )(q, k, v)


def _merge_kernel(x_ref, asb_ref, amla_ref, gl_ref, bg_ref, wosb_ref, womla_ref, wout_ref,
                  gpost_ref, o_ref):
    d = x_ref.shape[1]
    y_sb = jnp.dot(asb_ref[...], wosb_ref[...], preferred_element_type=F32)
    y_mla = jnp.dot(amla_ref[...], womla_ref[...], preferred_element_type=F32)
    gates = jax.nn.sigmoid(gl_ref[...].astype(F32) + bg_ref[...])
    mixed_in = gates[:, :d] * y_sb + gates[:, d:] * y_mla
    mixed = jnp.dot(mixed_in.astype(BF16), wout_ref[...], preferred_element_type=F32)
    o_ref[...] = x_ref[...] + _rms(mixed, gpost_ref[...])


def merge(x, a_sb, a_mla, gate_logits, b_gate, w_o_sb, w_o_mla, w_out, g_post, *, tm):
    m, d = x.shape
    tm = min(tm, m)
    const = lambda i: (0, 0)
    resident = dict(pipeline_mode=pl.Buffered(1))
    return pl.pallas_call(
        _merge_kernel,
        grid=(m // tm,),
        in_specs=[
            pl.BlockSpec((tm, d), lambda i: (i, 0)),
            pl.BlockSpec((tm, a_sb.shape[1]), lambda i: (i, 0)),
            pl.BlockSpec((tm, a_mla.shape[1]), lambda i: (i, 0)),
            pl.BlockSpec((tm, 2 * d), lambda i: (i, 0)),
            pl.BlockSpec(b_gate.shape, const),
            pl.BlockSpec(w_o_sb.shape, const, **resident),
            pl.BlockSpec(w_o_mla.shape, const, **resident),
            pl.BlockSpec(w_out.shape, const, **resident),
            pl.BlockSpec(g_post.shape, const),
        ],
        out_specs=pl.BlockSpec((tm, d), lambda i: (i, 0)),
        out_shape=jax.ShapeDtypeStruct((m, d), F32),
        compiler_params=_params("parallel"),
        name="merge",
    )(x, a_sb, a_mla, gate_logits, b_gate, w_o_sb, w_o_mla, w_out, g_post)


def _ffn_kernel(h_ref, gpre_ref, wg_ref, wu_ref, wd_ref, gpost_ref, o_ref, hn_ref, acc_ref):
    f = pl.program_id(1)

    @pl.when(f == 0)
    def _():
        hn_ref[...] = _rms(h_ref[...], gpre_ref[...]).astype(BF16)
        acc_ref[...] = jnp.zeros_like(acc_ref)

    hn = hn_ref[...]
    g = jnp.dot(hn, wg_ref[...], preferred_element_type=F32)
    up = jnp.dot(hn, wu_ref[...], preferred_element_type=F32)
    act = (g * jax.nn.sigmoid(g)) * up
    acc_ref[...] += jnp.dot(act.astype(BF16), wd_ref[...], preferred_element_type=F32)

    @pl.when(f == pl.num_programs(1) - 1)
    def _():
        o_ref[...] = h_ref[...] + _rms(acc_ref[...], gpost_ref[...])


def ffn(h, g_pre, w_gate_up, w_down, g_post, *, tm, tf):
    m, d = h.shape
    d_ff = w_down.shape[0]
    tm = min(tm, m)
    nf = d_ff // tf
    const = lambda i, f: (0, 0)
    return pl.pallas_call(
        _ffn_kernel,
        grid=(m // tm, nf),
        in_specs=[
            pl.BlockSpec((tm, d), lambda i, f: (i, 0)),
            pl.BlockSpec(g_pre.shape, const),
            pl.BlockSpec((d, tf), lambda i, f: (0, f)),
            pl.BlockSpec((d, tf), lambda i, f: (0, nf + f)),
            pl.BlockSpec((tf, d), lambda i, f: (f, 0)),
            pl.BlockSpec(g_post.shape, const),
        ],
        out_specs=pl.BlockSpec((tm, d), lambda i, f: (i, 0)),
        out_shape=jax.ShapeDtypeStruct((m, d), F32),
        scratch_shapes=[pltpu.VMEM((tm, d), BF16), pltpu.VMEM((tm, d), F32)],
        compiler_params=_params("parallel", "arbitrary"),
        name="ffn",
    )(h, g_pre, w_gate_up, w_gate_up, w_down, g_post)


def _rope_table(positions):
    half = MLA_ROPE_DIM // 2
    inv_freq = ROPE_THETA ** (-jnp.arange(half, dtype=F32) / half)
    ang = positions.astype(F32)[:, None] * inv_freq
    cos, sin = jnp.cos(ang), jnp.sin(ang)
    m = positions.shape[0]
    one = jnp.ones((m, LANES - MLA_ROPE_DIM), F32)
    zero = jnp.zeros((m, half), F32)
    zpad = jnp.zeros((m, LANES - MLA_ROPE_DIM), F32)
    return jnp.concatenate([cos, cos, one, -sin, zero, zpad, zero, sin, zpad], axis=1)


def _layer(l, x2, pos, norm_pre_mix, w_in, b_gate, q_norm, w_uq, kv_norm, w_ukv, w_o_sb, w_o_mla,
           w_out, norm_post_mix, norm_pre_ffn, w_gate_up, w_down, norm_post_ffn, *, batch, seq):
    d = x2.shape[1]
    sb_w = SB_HEADS * SB_HEAD_DIM
    o_qlat = 3 * sb_w
    o_kvlat = o_qlat + MLA_Q_RANK
    o_kr = o_kvlat + MLA_KV_RANK
    o_gate = o_kr + MLA_ROPE_DIM
    row = lambda v: v[l][None, :].astype(F32)

    w_in_l = w_in[l]
    w_qkv = w_in_l[:, :o_qlat].astype(BF16)
    w_gate = w_in_l[:, o_gate:].astype(BF16)
    w_lat = jnp.concatenate(
        [w_in_l[:, o_qlat:o_gate], jnp.zeros((d, LANES - MLA_ROPE_DIM), w_in_l.dtype)], axis=1).astype(BF16)

    qk_dim = MLA_NOPE_DIM + MLA_ROPE_DIM
    wq = w_uq[l].reshape(MLA_Q_RANK, MLA_HEADS, qk_dim)
    wq = jnp.pad(wq, ((0, 0), (0, 0), (0, MLA_QK_PAD - qk_dim)))
    wq = wq.reshape(MLA_Q_RANK, MLA_HEADS * MLA_QK_PAD).astype(BF16)
    wkv = w_ukv[l].reshape(MLA_KV_RANK, MLA_HEADS, MLA_NOPE_DIM + MLA_V_DIM)
    wkv = jnp.concatenate([wkv[:, :, :MLA_NOPE_DIM].reshape(MLA_KV_RANK, -1),
                           wkv[:, :, MLA_NOPE_DIM:].reshape(MLA_KV_RANK, -1)], axis=1).astype(BF16)

    g_pre = row(norm_pre_mix)
    sb_q_scale = -(SB_HEAD_DIM ** -0.5) * LOG2E
    cs_qkv = jnp.concatenate([jnp.full((1, sb_w), sb_q_scale, F32), jnp.ones((1, 2 * sb_w), F32)], axis=1)
    qkv = norm_matmul(x2, g_pre, w_qkv, cs_qkv, tm=1024, tn=1024)
    gate_logits = norm_matmul(x2, g_pre, w_gate, jnp.ones((1, w_gate.shape[1]), F32), tm=1024, tn=1024)

    q_m, k_m, v_m = mla_prep(x2, g_pre, w_lat, row(q_norm), wq, row(kv_norm), wkv, _rope_table(pos),
                             tm=512, q_scale=(qk_dim ** -0.5) * LOG2E)

    a_sb = sb_attention(qkv, batch=batch, seq=seq, tq=1024)
    a_mla = mla_attention(q_m, k_m, v_m, batch=batch, seq=seq, t=512)

    h1 = merge(x2, a_sb, a_mla, gate_logits, row(b_gate), w_o_sb[l].astype(BF16),
               w_o_mla[l].astype(BF16), w_out[l].astype(BF16), row(norm_post_mix), tm=256)
    return ffn(h1, row(norm_pre_ffn), w_gate_up[l].astype(BF16), w_down[l].astype(BF16),
               row(norm_post_ffn), tm=512, tf=512)


def kernel(x, positions, norm_pre_mix, w_in, b_gate, q_norm, w_uq, kv_norm, w_ukv, w_o_sb, w_o_mla,
           w_out, norm_post_mix, norm_pre_ffn, w_gate_up, w_down, norm_post_ffn):
    batch, seq, d = x.shape
    h = x.reshape(batch * seq, d)
    pos = positions.reshape(batch * seq)
    for l in range(w_in.shape[0]):
        h = _layer(l, h, pos, norm_pre_mix, w_in, b_gate, q_norm, w_uq, kv_norm, w_ukv, w_o_sb,
                   w_o_mla, w_out, norm_post_mix, norm_pre_ffn, w_gate_up, w_down, norm_post_ffn,
                   batch=batch, seq=seq)
    return h.reshape(batch, seq, d)
```

```python
import functools
import math

import jax
import jax.numpy as jnp
from jax import lax
from jax.experimental import pallas as pl
from jax.experimental.pallas import tpu as pltpu

F32 = jnp.float32
BF16 = jnp.bfloat16

CHUNK = 64
SB_HEADS = 8
SB_HEAD_DIM = 128
MLA_HEADS = 8
MLA_NOPE_DIM = 128
MLA_ROPE_DIM = 64
MLA_V_DIM = 128
MLA_Q_RANK = 768
MLA_KV_RANK = 512
ROPE_THETA = 10000.0
NORM_EPS = 1e-6
NEG_INF = -1e30
LOG2E = math.log2(math.e)

LANES = 128
MLA_QK_PAD = 2 * LANES
VMEM_LIMIT = 56 * 1024 * 1024
SB_BLOCK = 256
SB_DEAD_LOG2 = -160.0


def _params(*semantics):
    return pltpu.CompilerParams(dimension_semantics=semantics, vmem_limit_bytes=VMEM_LIMIT)


def _rms(x, g):
    ms = jnp.mean(x * x, axis=-1, keepdims=True)
    return x * lax.rsqrt(ms + NORM_EPS) * g


def _norm_matmul_kernel(x_ref, g_ref, w_ref, cs_ref, o_ref, u_ref):
    @pl.when(pl.program_id(1) == 0)
    def _():
        u_ref[...] = _rms(x_ref[...], g_ref[...]).astype(BF16)

    acc = jnp.dot(u_ref[...], w_ref[...], preferred_element_type=F32)
    o_ref[...] = (acc * cs_ref[...]).astype(o_ref.dtype)


def norm_matmul(x, gain, w, col_scale, *, tm, tn):
    m, d = x.shape
    n = w.shape[1]
    tm = min(tm, m)
    return pl.pallas_call(
        _norm_matmul_kernel,
        grid=(m // tm, n // tn),
        in_specs=[
            pl.BlockSpec((tm, d), lambda i, j: (i, 0)),
            pl.BlockSpec((1, d), lambda i, j: (0, 0)),
            pl.BlockSpec((d, tn), lambda i, j: (0, j)),
            pl.BlockSpec((1, tn), lambda i, j: (0, j)),
        ],
        out_specs=pl.BlockSpec((tm, tn), lambda i, j: (i, j)),
        out_shape=jax.ShapeDtypeStruct((m, n), BF16),
        scratch_shapes=[pltpu.VMEM((tm, d), BF16)],
        compiler_params=_params("parallel", "arbitrary"),
        name="norm_matmul",
    )(x, gain, w, col_scale)


def _rope128(r, rope_ref):
    c = rope_ref[:, 0:LANES]
    a = rope_ref[:, LANES:2 * LANES]
    b = rope_ref[:, 2 * LANES:3 * LANES]
    half = MLA_ROPE_DIM // 2
    return r * c + pltpu.roll(r, LANES - half, 1) * a + pltpu.roll(r, half, 1) * b


def _mla_prep_kernel(x_ref, g_ref, wlat_ref, qn_ref, wuq_ref, kvn_ref, wukv_ref, rope_ref,
                     q_ref, k_ref, v_ref, *, q_scale):
    u = _rms(x_ref[...], g_ref[...]).astype(BF16)
    lat = jnp.dot(u, wlat_ref[...], preferred_element_type=F32)
    q_lat = lat[:, :MLA_Q_RANK]
    kv_lat = lat[:, MLA_Q_RANK:MLA_Q_RANK + MLA_KV_RANK]
    k_r = lat[:, MLA_Q_RANK + MLA_KV_RANK:]

    q = jnp.dot(_rms(q_lat, qn_ref[...]).astype(BF16), wuq_ref[...], preferred_element_type=F32)
    kv = jnp.dot(_rms(kv_lat, kvn_ref[...]).astype(BF16), wukv_ref[...], preferred_element_type=F32)
    k_rope = _rope128(k_r, rope_ref).astype(BF16)

    kv_split = MLA_HEADS * MLA_NOPE_DIM
    for h in range(MLA_HEADS):
        lo = h * MLA_QK_PAD
        q_nope = q[:, lo:lo + LANES]
        q_rope = _rope128(q[:, lo + LANES:lo + 2 * LANES], rope_ref)
        q_ref[:, lo:lo + LANES] = (q_nope * q_scale).astype(BF16)
        q_ref[:, lo + LANES:lo + 2 * LANES] = (q_rope * q_scale).astype(BF16)
        k_ref[:, lo:lo + LANES] = kv[:, h * LANES:(h + 1) * LANES].astype(BF16)
        k_ref[:, lo + LANES:lo + 2 * LANES] = k_rope
    v_ref[...] = kv[:, kv_split:].astype(BF16)


def mla_prep(x, gain, w_lat, q_norm, w_uq, kv_norm, w_ukv, rope_tab, *, tm, q_scale):
    m, d = x.shape
    tm = min(tm, m)
    const = lambda i: (0, 0)
    qk_w = MLA_HEADS * MLA_QK_PAD
    v_w = MLA_HEADS * MLA_V_DIM
    return pl.pallas_call(
        functools.partial(_mla_prep_kernel, q_scale=q_scale),
        grid=(m // tm,),
        in_specs=[
            pl.BlockSpec((tm, d), lambda i: (i, 0)),
            pl.BlockSpec(gain.shape, const),
            pl.BlockSpec(w_lat.shape, const),
            pl.BlockSpec(q_norm.shape, const),
            pl.BlockSpec(w_uq.shape, const),
            pl.BlockSpec(kv_norm.shape, const),
            pl.BlockSpec(w_ukv.shape, const),
            pl.BlockSpec((tm, 3 * LANES), lambda i: (i, 0)),
        ],
        out_specs=[
            pl.BlockSpec((tm, qk_w), lambda i: (i, 0)),
            pl.BlockSpec((tm, qk_w), lambda i: (i, 0)),
            pl.BlockSpec((tm, v_w), lambda i: (i, 0)),
        ],
        out_shape=[
            jax.ShapeDtypeStruct((m, qk_w), BF16),
            jax.ShapeDtypeStruct((m, qk_w), BF16),
            jax.ShapeDtypeStruct((m, v_w), BF16),
        ],
        compiler_params=_params("parallel"),
        name="mla_prep",
    )(x, gain, w_lat, q_norm, w_uq, kv_norm, w_ukv, rope_tab)


def _qkt(q, k):
    return lax.dot_general(q, k, (((1,), (1,)), ((), ())), preferred_element_type=F32)


def _sb_attn_kernel(q_ref, k_ref, v_ref, o_ref, run_ref, acc_ref, *, tq):
    tb = SB_BLOCK
    i = pl.program_id(2)
    subs = range(tq // tb)
    row = lax.broadcasted_iota(jnp.int32, (tb, tb), 0)
    col = lax.broadcasted_iota(jnp.int32, (tb, tb), 1)
    before = row > col
    later = before.astype(BF16)

    def log2_keep(nz):
        return jnp.minimum(nz, 0.0) - jnp.log(1.0 + jnp.exp2(-jnp.abs(nz))) * LOG2E

    def suffix(keep):
        return jnp.dot(keep.astype(BF16), later, preferred_element_type=F32)

    q_first = [i * tq + g * tb for g in subs]
    starts = [pl.multiple_of(jnp.maximum(q_first[g] - tb, 0), tb) for g in subs]
    nz = [_qkt(q_ref[g * tb:(g + 1) * tb, :], k_ref[pl.ds(starts[g], 2 * tb), :]) for g in subs]
    nz_lo = [x[:, :tb] for x in nz]
    nz_hi = [x[:, tb:] for x in nz]
    delta = q_first[0] - starts[0]
    valid_lo = [(col - row) < delta] + [None for _ in subs[1:]]
    valid_hi = [(col - row) < delta - tb] + [before for _ in subs[1:]]
    keep_lo = [log2_keep(x) for x in nz_lo]
    keep_lo[0] = jnp.where(valid_lo[0], keep_lo[0], 0.0)
    keep_hi = [jnp.where(valid_hi[g], log2_keep(nz_hi[g]), 0.0) for g in subs]
    suf_hi = [suffix(x) for x in keep_hi]
    suf_lo = [suffix(x) for x in keep_lo]
    run_lo = [suf_hi[g][:, 0:1] + keep_hi[g][:, 0:1] for g in subs]
    w_hi = [jnp.where(valid_hi[g], jnp.exp2((keep_hi[g] - nz_hi[g]) + suf_hi[g]), 0.0) for g in subs]
    w_lo = [jnp.exp2((keep_lo[g] - nz_lo[g]) + suf_lo[g] + run_lo[g]) for g in subs]
    w_lo[0] = jnp.where(valid_lo[0], w_lo[0], 0.0)
    for g in subs:
        w = jnp.concatenate([w_lo[g], w_hi[g]], axis=1).astype(BF16)
        acc_ref[g * tb:(g + 1) * tb, :] = jnp.dot(w, v_ref[pl.ds(starts[g], 2 * tb), :],
                                                  preferred_element_type=F32)
        run_ref[g * tb:(g + 1) * tb, :] = run_lo[g] + suf_lo[g][:, 0:1] + keep_lo[g][:, 0:1]

    def walk_back(g):
        rows = pl.ds(g * tb, tb)

        def alive(c):
            return jnp.logical_and(c[0] >= 0, c[1] > SB_DEAD_LOG2)

        def body(c):
            start = pl.multiple_of(c[0] * tb, tb)
            nz = _qkt(q_ref[rows, :], k_ref[pl.ds(start, tb), :])
            keep = log2_keep(nz)
            suf = suffix(keep)
            run = run_ref[rows, :]
            w = jnp.exp2((keep - nz) + suf + run)
            acc_ref[rows, :] += jnp.dot(w.astype(BF16), v_ref[pl.ds(start, tb), :],
                                        preferred_element_type=F32)
            run = run + suf[:, 0:1] + keep[:, 0:1]
            run_ref[rows, :] = run
            return c[0] - 1, jnp.max(run)

        lax.while_loop(alive, body, (i * len(subs) + g - 2, jnp.max(run_ref[rows, :])))

    @pl.when(jnp.max(run_ref[...]) > SB_DEAD_LOG2)
    def _():
        for g in subs:
            walk_back(g)

    o_ref[...] = acc_ref[...].astype(o_ref.dtype)


def sb_attention(qkv, *, batch, seq, tq):
    m = qkv.shape[0]
    d = SB_HEAD_DIM
    tq = min(tq, seq)
    assert tq % SB_BLOCK == 0 and seq >= 2 * SB_BLOCK
    nq = seq // tq
    return pl.pallas_call(
        functools.partial(_sb_attn_kernel, tq=tq),
        grid=(batch, SB_HEADS, nq),
        in_specs=[
            pl.BlockSpec((tq, d), lambda b, h, i: (b * nq + i, h)),
            pl.BlockSpec((seq, d), lambda b, h, i: (b, SB_HEADS + h)),
            pl.BlockSpec((seq, d), lambda b, h, i: (b, 2 * SB_HEADS + h)),
        ],
        out_specs=pl.BlockSpec((tq, d), lambda b, h, i: (b * nq + i, h)),
        out_shape=jax.ShapeDtypeStruct((m, SB_HEADS * d), BF16),
        scratch_shapes=[pltpu.VMEM((tq, 1), F32), pltpu.VMEM((tq, d), F32)],
        compiler_params=_params("parallel", "parallel", "arbitrary"),
        name="sb_attention",
    )(qkv, qkv, qkv)


def _mla_attn_kernel(q_ref, k_ref, vt_ref, o_ref, s_ref, acc_ref, *, tq, tk):
    i = pl.program_id(2)
    q = q_ref[...]
    per_q = tq // tk
    assert per_q == 2

    def scores(blk, slot):
        keys = pl.ds(pl.multiple_of(blk * tk, tk), tk)
        s_ref[slot] = _qkt(k_ref[keys, :], q)

    def update(blk, slot, carry, mask):
        m_old, l = carry
        s = s_ref[slot]
        if mask is not None:
            s = jnp.where(mask, s, NEG_INF)
        m_new = jnp.maximum(m_old, jnp.max(s, axis=0, keepdims=True))
        alpha = jnp.exp2(m_old - m_new)
        p = jnp.exp2(s - m_new)
        l = alpha * l + jnp.sum(p, axis=0, keepdims=True)
        pv = jnp.dot(vt_ref[blk], p.astype(BF16), preferred_element_type=F32)
        acc_ref[...] = alpha * acc_ref[...] + pv
        return m_new, l

    acc_ref[...] = jnp.zeros(acc_ref.shape, F32)
    carry = (jnp.full((1, tq), NEG_INF, F32), jnp.zeros((1, tq), F32))
    scores(0, 0)

    def body(t, carry):
        scores(2 * t + 1, 1)
        carry = update(2 * t, 0, carry, None)
        scores(2 * t + 2, 0)
        return update(2 * t + 1, 1, carry, None)

    carry = lax.fori_loop(0, i, body, carry)
    scores(2 * i + 1, 1)
    k_chunk = lax.broadcasted_iota(jnp.int32, (tk, tq), 0) // CHUNK
    q_chunk = lax.broadcasted_iota(jnp.int32, (tk, tq), 1) // CHUNK
    carry = update(2 * i, 0, carry, k_chunk <= q_chunk)
    m, l = update(2 * i + 1, 1, carry, k_chunk + tk // CHUNK <= q_chunk)
    o_ref[...] = (acc_ref[...] / l).T.astype(o_ref.dtype)


def mla_attention(q, k, v, *, batch, seq, tq):
    m = q.shape[0]
    tq = min(tq, seq)
    tk = tq // 2
    assert tk % CHUNK == 0
    nq = seq // tq
    nk = seq // tk
    vt = v.reshape(batch, nk, tk, MLA_HEADS, MLA_V_DIM).transpose(0, 3, 1, 4, 2)
    vt = vt.reshape(batch * MLA_HEADS, nk, MLA_V_DIM, tk)
    return pl.pallas_call(
        functools.partial(_mla_attn_kernel, tq=tq, tk=tk),
        grid=(batch, MLA_HEADS, nq),
        in_specs=[
            pl.BlockSpec((tq, MLA_QK_PAD), lambda b, h, i: (b * nq + i, h)),
            pl.BlockSpec((seq, MLA_QK_PAD), lambda b, h, i: (b, h)),
            pl.BlockSpec((None, nk, MLA_V_DIM, tk), lambda b, h, i: (b * MLA_HEADS + h, 0, 0, 0)),
        ],
        out_specs=pl.BlockSpec((tq, MLA_V_DIM), lambda b, h, i: (b * nq + i, h)),
        out_shape=jax.ShapeDtypeStruct((m, MLA_HEADS * MLA_V_DIM), BF16),
        scratch_shapes=[pltpu.VMEM((2, tk, tq), F32), pltpu.VMEM((MLA_V_DIM, tq), F32)],
        compiler_params=_params("parallel", "parallel", "arbitrary"),
        name="mla_attention",
    )(q, k, vt)


def _merge_kernel(x_ref, asb_ref, amla_ref, gl_ref, bg_ref, wosb_ref, womla_ref, wout_ref,
                  gpost_ref, o_ref):
    d = x_ref.shape[1]
    y_sb = jnp.dot(asb_ref[...], wosb_ref[...], preferred_element_type=F32)
    y_mla = jnp.dot(amla_ref[...], womla_ref[...], preferred_element_type=F32)
    gates = jax.nn.sigmoid(gl_ref[...].astype(F32) + bg_ref[...])
    mixed_in = gates[:, :d] * y_sb + gates[:, d:] * y_mla
    mixed = jnp.dot(mixed_in.astype(BF16), wout_ref[...], preferred_element_type=F32)
    o_ref[...] = x_ref[...] + _rms(mixed, gpost_ref[...])


def merge(x, a_sb, a_mla, gate_logits, b_gate, w_o_sb, w_o_mla, w_out, g_post, *, tm):
    m, d = x.shape
    tm = min(tm, m)
    const = lambda i: (0, 0)
    resident = dict(pipeline_mode=pl.Buffered(1))
    return pl.pallas_call(
        _merge_kernel,
        grid=(m // tm,),
        in_specs=[
            pl.BlockSpec((tm, d), lambda i: (i, 0)),
            pl.BlockSpec((tm, a_sb.shape[1]), lambda i: (i, 0)),
            pl.BlockSpec((tm, a_mla.shape[1]), lambda i: (i, 0)),
            pl.BlockSpec((tm, 2 * d), lambda i: (i, 0)),
            pl.BlockSpec(b_gate.shape, const),
            pl.BlockSpec(w_o_sb.shape, const, **resident),
            pl.BlockSpec(w_o_mla.shape, const, **resident),
            pl.BlockSpec(w_out.shape, const, **resident),
            pl.BlockSpec(g_post.shape, const),
        ],
        out_specs=pl.BlockSpec((tm, d), lambda i: (i, 0)),
        out_shape=jax.ShapeDtypeStruct((m, d), F32),
        compiler_params=_params("parallel"),
        name="merge",
    )(x, a_sb, a_mla, gate_logits, b_gate, w_o_sb, w_o_mla, w_out, g_post)


def _ffn_kernel(h_ref, gpre_ref, wg_ref, wu_ref, wd_ref, gpost_ref, o_ref, hn_ref, acc_ref):
    f = pl.program_id(1)

    @pl.when(f == 0)
    def _():
        hn_ref[...] = _rms(h_ref[...], gpre_ref[...]).astype(BF16)
        acc_ref[...] = jnp.zeros_like(acc_ref)

    hn = hn_ref[...]
    g = jnp.dot(hn, wg_ref[...], preferred_element_type=F32)
    up = jnp.dot(hn, wu_ref[...], preferred_element_type=F32)
    act = (g * jax.nn.sigmoid(g)) * up
    acc_ref[...] += jnp.dot(act.astype(BF16), wd_ref[...], preferred_element_type=F32)

    @pl.when(f == pl.num_programs(1) - 1)
    def _():
        o_ref[...] = h_ref[...] + _rms(acc_ref[...], gpost_ref[...])


def ffn(h, g_pre, w_gate_up, w_down, g_post, *, tm, tf):
    m, d = h.shape
    d_ff = w_down.shape[0]
    tm = min(tm, m)
    nf = d_ff // tf
    const = lambda i, f: (0, 0)
    return pl.pallas_call(
        _ffn_kernel,
        grid=(m // tm, nf),
        in_specs=[
            pl.BlockSpec((tm, d), lambda i, f: (i, 0)),
            pl.BlockSpec(g_pre.shape, const),
            pl.BlockSpec((d, tf), lambda i, f: (0, f)),
            pl.BlockSpec((d, tf), lambda i, f: (0, nf + f)),
            pl.BlockSpec((tf, d), lambda i, f: (f, 0)),
            pl.BlockSpec(g_post.shape, const),
        ],
        out_specs=pl.BlockSpec((tm, d), lambda i, f: (i, 0)),
        out_shape=jax.ShapeDtypeStruct((m, d), F32),
        scratch_shapes=[pltpu.VMEM((tm, d), BF16), pltpu.VMEM((tm, d), F32)],
        compiler_params=_params("parallel", "arbitrary"),
        name="ffn",
    )(h, g_pre, w_gate_up, w_gate_up, w_down, g_post)


def _rope_table(positions):
    half = MLA_ROPE_DIM // 2
    inv_freq = ROPE_THETA ** (-jnp.arange(half, dtype=F32) / half)
    ang = positions.astype(F32)[:, None] * inv_freq
    cos, sin = jnp.cos(ang), jnp.sin(ang)
    m = positions.shape[0]
    one = jnp.ones((m, LANES - MLA_ROPE_DIM), F32)
    zero = jnp.zeros((m, half), F32)
    zpad = jnp.zeros((m, LANES - MLA_ROPE_DIM), F32)
    return jnp.concatenate([cos, cos, one, -sin, zero, zpad, zero, sin, zpad], axis=1)


def _layer(l, x2, pos, norm_pre_mix, w_in, b_gate, q_norm, w_uq, kv_norm, w_ukv, w_o_sb, w_o_mla,
           w_out, norm_post_mix, norm_pre_ffn, w_gate_up, w_down, norm_post_ffn, *, batch, seq):
    d = x2.shape[1]
    sb_w = SB_HEADS * SB_HEAD_DIM
    o_qlat = 3 * sb_w
    o_kvlat = o_qlat + MLA_Q_RANK
    o_kr = o_kvlat + MLA_KV_RANK
    o_gate = o_kr + MLA_ROPE_DIM
    row = lambda v: v[l][None, :].astype(F32)

    w_in_l = w_in[l]
    w_qkv = w_in_l[:, :o_qlat].astype(BF16)
    w_gate = w_in_l[:, o_gate:].astype(BF16)
    w_lat = jnp.concatenate(
        [w_in_l[:, o_qlat:o_gate], jnp.zeros((d, LANES - MLA_ROPE_DIM), w_in_l.dtype)], axis=1).astype(BF16)

    qk_dim = MLA_NOPE_DIM + MLA_ROPE_DIM
    wq = w_uq[l].reshape(MLA_Q_RANK, MLA_HEADS, qk_dim)
    wq = jnp.pad(wq, ((0, 0), (0, 0), (0, MLA_QK_PAD - qk_dim)))
    wq = wq.reshape(MLA_Q_RANK, MLA_HEADS * MLA_QK_PAD).astype(BF16)
    wkv = w_ukv[l].reshape(MLA_KV_RANK, MLA_HEADS, MLA_NOPE_DIM + MLA_V_DIM)
    wkv = jnp.concatenate([wkv[:, :, :MLA_NOPE_DIM].reshape(MLA_KV_RANK, -1),
                           wkv[:, :, MLA_NOPE_DIM:].reshape(MLA_KV_RANK, -1)], axis=1).astype(BF16)

    g_pre = row(norm_pre_mix)
    sb_q_scale = -(SB_HEAD_DIM ** -0.5) * LOG2E
    cs_qkv = jnp.concatenate([jnp.full((1, sb_w), sb_q_scale, F32), jnp.ones((1, 2 * sb_w), F32)], axis=1)
    qkv = norm_matmul(x2, g_pre, w_qkv, cs_qkv, tm=1024, tn=1024)
    gate_logits = norm_matmul(x2, g_pre, w_gate, jnp.ones((1, w_gate.shape[1]), F32), tm=1024, tn=1024)

    q_m, k_m, v_m = mla_prep(x2, g_pre, w_lat, row(q_norm), wq, row(kv_norm), wkv, _rope_table(pos),
                             tm=512, q_scale=(qk_dim ** -0.5) * LOG2E)

    a_sb = sb_attention(qkv, batch=batch, seq=seq, tq=1024)
    a_mla = mla_attention(q_m, k_m, v_m, batch=batch, seq=seq, tq=1024)

    h1 = merge(x2, a_sb, a_mla, gate_logits, row(b_gate), w_o_sb[l].astype(BF16),
               w_o_mla[l].astype(BF16), w_out[l].astype(BF16), row(norm_post_mix), tm=256)
    return ffn(h1, row(norm_pre_ffn), w_gate_up[l].astype(BF16), w_down[l].astype(BF16),
               row(norm_post_ffn), tm=512, tf=512)


def kernel(x, positions, norm_pre_mix, w_in, b_gate, q_norm, w_uq, kv_norm, w_ukv, w_o_sb, w_o_mla,
           w_out, norm_post_mix, norm_pre_ffn, w_gate_up, w_down, norm_post_ffn):
    batch, seq, d = x.shape
    h = x.reshape(batch * seq, d)
    pos = positions.reshape(batch * seq)
    for l in range(w_in.shape[0]):
        h = _layer(l, h, pos, norm_pre_mix, w_in, b_gate, q_norm, w_uq, kv_norm, w_ukv, w_o_sb,
                   w_o_mla, w_out, norm_post_mix, norm_pre_ffn, w_gate_up, w_down, norm_post_ffn,
                   batch=batch, seq=seq)
    return h.reshape(batch, seq, d)
```

```python
import functools
import math

import jax
import jax.numpy as jnp
from jax import lax
from jax.experimental import pallas as pl
from jax.experimental.pallas import tpu as pltpu

F32 = jnp.float32
BF16 = jnp.bfloat16

CHUNK = 64
SB_HEADS = 8
SB_HEAD_DIM = 128
MLA_HEADS = 8
MLA_NOPE_DIM = 128
MLA_ROPE_DIM = 64
MLA_V_DIM = 128
MLA_Q_RANK = 768
MLA_KV_RANK = 512
ROPE_THETA = 10000.0
NORM_EPS = 1e-6
NEG_INF = -1e30
LOG2E = math.log2(math.e)

LANES = 128
MLA_QK_PAD = 2 * LANES
VMEM_LIMIT = 56 * 1024 * 1024
SB_BLOCK = 256
SB_DEAD_LOG2 = -160.0
BF16_ROWS = 16
MLA_KEY_BLOCK = 512


def _params(*semantics):
    return pltpu.CompilerParams(dimension_semantics=semantics, vmem_limit_bytes=VMEM_LIMIT)


def _rms(x, g):
    ms = jnp.mean(x * x, axis=-1, keepdims=True)
    return x * lax.rsqrt(ms + NORM_EPS) * g


def _norm_matmul_kernel(x_ref, g_ref, w_ref, cs_ref, o_ref, u_ref):
    @pl.when(pl.program_id(1) == 0)
    def _():
        u_ref[...] = _rms(x_ref[...], g_ref[...]).astype(BF16)

    acc = jnp.dot(u_ref[...], w_ref[...], preferred_element_type=F32)
    o_ref[...] = (acc * cs_ref[...]).astype(o_ref.dtype)


def norm_matmul(x, gain, w, col_scale, *, tm, tn):
    m, d = x.shape
    n = w.shape[1]
    tm = min(tm, m)
    return pl.pallas_call(
        _norm_matmul_kernel,
        grid=(m // tm, n // tn),
        in_specs=[
            pl.BlockSpec((tm, d), lambda i, j: (i, 0)),
            pl.BlockSpec((1, d), lambda i, j: (0, 0)),
            pl.BlockSpec((d, tn), lambda i, j: (0, j)),
            pl.BlockSpec((1, tn), lambda i, j: (0, j)),
        ],
        out_specs=pl.BlockSpec((tm, tn), lambda i, j: (i, j)),
        out_shape=jax.ShapeDtypeStruct((m, n), BF16),
        scratch_shapes=[pltpu.VMEM((tm, d), BF16)],
        compiler_params=_params("parallel", "arbitrary"),
        name="norm_matmul",
    )(x, gain, w, col_scale)


def _rope_coeffs(rope_ref):
    half = MLA_ROPE_DIM // 2
    t = rope_ref[...]
    lane = lax.broadcasted_iota(jnp.int32, t.shape, 1)
    first, second = lane < half, jnp.logical_and(lane >= half, lane < 2 * half)
    c = jnp.where(first, t, jnp.where(second, pltpu.roll(t, half, 1), 1.0))
    a = jnp.where(first, -pltpu.roll(t, LANES - half, 1), 0.0)
    b = jnp.where(second, t, 0.0)
    return c, a, b


def _rope128(r, coeffs):
    c, a, b = coeffs
    half = MLA_ROPE_DIM // 2
    return r * c + pltpu.roll(r, LANES - half, 1) * a + pltpu.roll(r, half, 1) * b


def _mla_prep_kernel(x_ref, g_ref, wlat_ref, qn_ref, wuq_ref, kvn_ref, wukv_ref, rope_ref,
                     q_ref, k_ref, vt_ref, *, q_scale):
    u = _rms(x_ref[...], g_ref[...]).astype(BF16)
    lat = jnp.dot(u, wlat_ref[...], preferred_element_type=F32)
    q_lat = lat[:, :MLA_Q_RANK]
    kv_lat = lat[:, MLA_Q_RANK:MLA_Q_RANK + MLA_KV_RANK]
    k_r = lat[:, MLA_Q_RANK + MLA_KV_RANK:]

    q = jnp.dot(_rms(q_lat, qn_ref[...]).astype(BF16), wuq_ref[...], preferred_element_type=F32)
    kv = jnp.dot(_rms(kv_lat, kvn_ref[...]).astype(BF16), wukv_ref[...], preferred_element_type=F32)
    coeffs = _rope_coeffs(rope_ref)
    k_rope = _rope128(k_r, coeffs).astype(BF16)
    ones = jnp.ones((BF16_ROWS, x_ref.shape[0]), BF16)

    kv_split = MLA_HEADS * MLA_NOPE_DIM
    for h in range(MLA_HEADS):
        lo = h * MLA_QK_PAD
        q_nope = q[:, lo:lo + LANES]
        q_rope = _rope128(q[:, lo + LANES:lo + 2 * LANES], coeffs)
        q_ref[:, lo:lo + LANES] = (q_nope * q_scale).astype(BF16)
        q_ref[:, lo + LANES:lo + 2 * LANES] = (q_rope * q_scale).astype(BF16)
        k_ref[:, lo:lo + LANES] = kv[:, h * LANES:(h + 1) * LANES].astype(BF16)
        k_ref[:, lo + LANES:lo + 2 * LANES] = k_rope
        v_h = kv[:, kv_split + h * MLA_V_DIM:kv_split + (h + 1) * MLA_V_DIM]
        vt_ref[h, :MLA_V_DIM, :] = v_h.T.astype(BF16)
        vt_ref[h, MLA_V_DIM:, :] = ones


def mla_prep(x, gain, w_lat, q_norm, w_uq, kv_norm, w_ukv, rope_tab, *, batch, tm, q_scale):
    m, d = x.shape
    n_tiles = m // batch // tm
    const = lambda i: (0, 0)
    qk_w = MLA_HEADS * MLA_QK_PAD
    d_ext = MLA_V_DIM + BF16_ROWS
    return pl.pallas_call(
        functools.partial(_mla_prep_kernel, q_scale=q_scale),
        grid=(m // tm,),
        in_specs=[
            pl.BlockSpec((tm, d), lambda i: (i, 0)),
            pl.BlockSpec(gain.shape, const),
            pl.BlockSpec(w_lat.shape, const),
            pl.BlockSpec(q_norm.shape, const),
            pl.BlockSpec(w_uq.shape, const),
            pl.BlockSpec(kv_norm.shape, const),
            pl.BlockSpec(w_ukv.shape, const),
            pl.BlockSpec((tm, LANES), lambda i: (i, 0)),
        ],
        out_specs=[
            pl.BlockSpec((tm, qk_w), lambda i: (i, 0)),
            pl.BlockSpec((tm, qk_w), lambda i: (i, 0)),
            pl.BlockSpec((None, MLA_HEADS, None, d_ext, tm), lambda i: (i // n_tiles, 0, i % n_tiles, 0, 0)),
        ],
        out_shape=[
            jax.ShapeDtypeStruct((m, qk_w), BF16),
            jax.ShapeDtypeStruct((m, qk_w), BF16),
            jax.ShapeDtypeStruct((batch, MLA_HEADS, n_tiles, d_ext, tm), BF16),
        ],
        compiler_params=_params("parallel"),
        name="mla_prep",
    )(x, gain, w_lat, q_norm, w_uq, kv_norm, w_ukv, rope_tab)


def _qkt(q, k):
    return lax.dot_general(q, k, (((1,), (1,)), ((), ())), preferred_element_type=F32)


def _sb_attn_kernel(q_ref, k_ref, v_ref, o_ref, run_ref, acc_ref, *, tq):
    tb = SB_BLOCK
    i = pl.program_id(2)
    subs = range(tq // tb)
    row = lax.broadcasted_iota(jnp.int32, (tb, tb), 0)
    col = lax.broadcasted_iota(jnp.int32, (tb, tb), 1)
    before = row > col
    later = before.astype(BF16)

    def log2_keep(nz):
        return jnp.minimum(nz, 0.0) - jnp.log(1.0 + jnp.exp2(-jnp.abs(nz))) * LOG2E

    def suffix(keep):
        return jnp.dot(keep.astype(BF16), later, preferred_element_type=F32)

    q_first = [i * tq + g * tb for g in subs]
    starts = [pl.multiple_of(jnp.maximum(q_first[g] - tb, 0), tb) for g in subs]
    nz = [_qkt(q_ref[g * tb:(g + 1) * tb, :], k_ref[pl.ds(starts[g], 2 * tb), :]) for g in subs]
    nz_lo = [x[:, :tb] for x in nz]
    nz_hi = [x[:, tb:] for x in nz]
    delta = q_first[0] - starts[0]
    valid_lo = [(col - row) < delta] + [None for _ in subs[1:]]
    valid_hi = [(col - row) < delta - tb] + [before for _ in subs[1:]]
    keep_lo = [log2_keep(x) for x in nz_lo]
    keep_lo[0] = jnp.where(valid_lo[0], keep_lo[0], 0.0)
    keep_hi = [jnp.where(valid_hi[g], log2_keep(nz_hi[g]), 0.0) for g in subs]
    suf_hi = [suffix(x) for x in keep_hi]
    suf_lo = [suffix(x) for x in keep_lo]
    run_lo = [suf_hi[g][:, 0:1] + keep_hi[g][:, 0:1] for g in subs]
    w_hi = [jnp.where(valid_hi[g], jnp.exp2((keep_hi[g] - nz_hi[g]) + suf_hi[g]), 0.0) for g in subs]
    w_lo = [jnp.exp2((keep_lo[g] - nz_lo[g]) + suf_lo[g] + run_lo[g]) for g in subs]
    w_lo[0] = jnp.where(valid_lo[0], w_lo[0], 0.0)
    for g in subs:
        w = jnp.concatenate([w_lo[g], w_hi[g]], axis=1).astype(BF16)
        acc_ref[g * tb:(g + 1) * tb, :] = jnp.dot(w, v_ref[pl.ds(starts[g], 2 * tb), :],
                                                  preferred_element_type=F32)
        run_ref[g * tb:(g + 1) * tb, :] = run_lo[g] + suf_lo[g][:, 0:1] + keep_lo[g][:, 0:1]

    def walk_back(g):
        rows = pl.ds(g * tb, tb)

        def alive(c):
            return jnp.logical_and(c[0] >= 0, c[1] > SB_DEAD_LOG2)

        def body(c):
            start = pl.multiple_of(c[0] * tb, tb)
            nz = _qkt(q_ref[rows, :], k_ref[pl.ds(start, tb), :])
            keep = log2_keep(nz)
            suf = suffix(keep)
            run = run_ref[rows, :]
            w = jnp.exp2((keep - nz) + suf + run)
            acc_ref[rows, :] += jnp.dot(w.astype(BF16), v_ref[pl.ds(start, tb), :],
                                        preferred_element_type=F32)
            run = run + suf[:, 0:1] + keep[:, 0:1]
            run_ref[rows, :] = run
            return c[0] - 1, jnp.max(run)

        lax.while_loop(alive, body, (i * len(subs) + g - 2, jnp.max(run_ref[rows, :])))

    @pl.when(jnp.max(run_ref[...]) > SB_DEAD_LOG2)
    def _():
        for g in subs:
            walk_back(g)

    o_ref[...] = acc_ref[...].astype(o_ref.dtype)


def sb_attention(qkv, *, batch, seq, tq, first_col):
    m = qkv.shape[0]
    d = SB_HEAD_DIM
    tq = min(tq, seq)
    assert tq % SB_BLOCK == 0 and seq >= 2 * SB_BLOCK
    nq = seq // tq
    return pl.pallas_call(
        functools.partial(_sb_attn_kernel, tq=tq),
        grid=(batch, SB_HEADS, nq),
        in_specs=[
            pl.BlockSpec((tq, d), lambda b, h, i: (b * nq + i, first_col + h)),
            pl.BlockSpec((seq, d), lambda b, h, i: (b, first_col + SB_HEADS + h)),
            pl.BlockSpec((seq, d), lambda b, h, i: (b, first_col + 2 * SB_HEADS + h)),
        ],
        out_specs=pl.BlockSpec((tq, d), lambda b, h, i: (b * nq + i, h)),
        out_shape=jax.ShapeDtypeStruct((m, SB_HEADS * d), BF16),
        scratch_shapes=[pltpu.VMEM((tq, 1), F32), pltpu.VMEM((tq, d), F32)],
        compiler_params=_params("parallel", "parallel", "arbitrary"),
        name="sb_attention",
    )(qkv, qkv, qkv)


def _mla_attn_kernel(q_ref, k_ref, vt_ref, o_ref, s_ref, smax_ref, acc_ref, *, tq, tk):
    i = pl.program_id(2)
    q = q_ref[...]
    assert tq == 2 * tk

    def scores(blk, slot):
        keys = pl.ds(pl.multiple_of(blk * tk, tk), tk)
        s = _qkt(k_ref[keys, :], q)
        s_ref[slot] = s
        smax_ref[slot] = jnp.max(s, axis=0, keepdims=True)

    def update(blk, slot, m_old, mask):
        s = s_ref[slot]
        if mask is None:
            s_max = smax_ref[slot]
        else:
            s = jnp.where(mask, s, NEG_INF)
            s_max = jnp.max(s, axis=0, keepdims=True)
        m_new = jnp.maximum(m_old, s_max)
        p = jnp.exp2(s - m_new).astype(BF16)
        pv = jnp.dot(vt_ref[blk], p, preferred_element_type=F32)
        acc_ref[...] = jnp.exp2(m_old - m_new) * acc_ref[...] + pv
        return m_new

    acc_ref[...] = jnp.zeros(acc_ref.shape, F32)
    m = jnp.full((1, tq), NEG_INF, F32)
    scores(0, 0)

    def body(t, m):
        scores(2 * t + 1, 1)
        m = update(2 * t, 0, m, None)
        scores(2 * t + 2, 0)
        return update(2 * t + 1, 1, m, None)

    m = lax.fori_loop(0, i, body, m)
    scores(2 * i + 1, 1)
    k_chunk = lax.broadcasted_iota(jnp.int32, (tk, tq), 0) // CHUNK
    q_chunk = lax.broadcasted_iota(jnp.int32, (tk, tq), 1) // CHUNK
    m = update(2 * i, 0, m, k_chunk <= q_chunk)
    update(2 * i + 1, 1, m, k_chunk + tk // CHUNK <= q_chunk)
    out_t = acc_ref[:MLA_V_DIM, :] / acc_ref[MLA_V_DIM:MLA_V_DIM + 1, :]
    o_ref[...] = out_t.T.astype(o_ref.dtype)


def mla_attention(q, k, vt, *, batch, seq):
    m = q.shape[0]
    _, _, nk, d_ext, tk = vt.shape
    tq = 2 * tk
    assert tk % CHUNK == 0 and nk * tk == seq
    nq = seq // tq
    return pl.pallas_call(
        functools.partial(_mla_attn_kernel, tq=tq, tk=tk),
        grid=(batch, MLA_HEADS, nq),
        in_specs=[
            pl.BlockSpec((tq, MLA_QK_PAD), lambda b, h, i: (b * nq + i, h)),
            pl.BlockSpec((seq, MLA_QK_PAD), lambda b, h, i: (b, h)),
            pl.BlockSpec((None, None, nk, d_ext, tk), lambda b, h, i: (b, h, 0, 0, 0)),
        ],
        out_specs=pl.BlockSpec((tq, MLA_V_DIM), lambda b, h, i: (b * nq + i, h)),
        out_shape=jax.ShapeDtypeStruct((m, MLA_HEADS * MLA_V_DIM), BF16),
        scratch_shapes=[pltpu.VMEM((2, tk, tq), F32), pltpu.VMEM((2, 1, tq), F32),
                        pltpu.VMEM((d_ext, tq), F32)],
        compiler_params=_params("parallel", "parallel", "arbitrary"),
        name="mla_attention",
    )(q, k, vt)


def _merge_kernel(x_ref, asb_ref, amla_ref, gl_ref, bg_ref, wosb_ref, womla_ref, wout_ref,
                  gpost_ref, o_ref):
    d = x_ref.shape[1]
    y_sb = jnp.dot(asb_ref[...], wosb_ref[...], preferred_element_type=F32)
    y_mla = jnp.dot(amla_ref[...], womla_ref[...], preferred_element_type=F32)
    gates = jax.nn.sigmoid(gl_ref[...].astype(F32) + bg_ref[...])
    mixed_in = gates[:, :d] * y_sb + gates[:, d:] * y_mla
    mixed = jnp.dot(mixed_in.astype(BF16), wout_ref[...], preferred_element_type=F32)
    o_ref[...] = x_ref[...] + _rms(mixed, gpost_ref[...])


def merge(x, a_sb, a_mla, gate_logits, b_gate, w_o_sb, w_o_mla, w_out, g_post, *, tm):
    m, d = x.shape
    tm = min(tm, m)
    const = lambda i: (0, 0)
    resident = dict(pipeline_mode=pl.Buffered(1))
    return pl.pallas_call(
        _merge_kernel,
        grid=(m // tm,),
        in_specs=[
            pl.BlockSpec((tm, d), lambda i: (i, 0)),
            pl.BlockSpec((tm, a_sb.shape[1]), lambda i: (i, 0)),
            pl.BlockSpec((tm, a_mla.shape[1]), lambda i: (i, 0)),
            pl.BlockSpec((tm, 2 * d), lambda i: (i, 0)),
            pl.BlockSpec(b_gate.shape, const),
            pl.BlockSpec(w_o_sb.shape, const, **resident),
            pl.BlockSpec(w_o_mla.shape, const, **resident),
            pl.BlockSpec(w_out.shape, const, **resident),
            pl.BlockSpec(g_post.shape, const),
        ],
        out_specs=pl.BlockSpec((tm, d), lambda i: (i, 0)),
        out_shape=jax.ShapeDtypeStruct((m, d), F32),
        compiler_params=_params("parallel"),
        name="merge",
    )(x, a_sb, a_mla, gate_logits, b_gate, w_o_sb, w_o_mla, w_out, g_post)


def _ffn_kernel(h_ref, gpre_ref, wg_ref, wu_ref, wd_ref, gpost_ref, o_ref, hn_ref, acc_ref):
    f = pl.program_id(1)

    @pl.when(f == 0)
    def _():
        hn_ref[...] = _rms(h_ref[...], gpre_ref[...]).astype(BF16)
        acc_ref[...] = jnp.zeros_like(acc_ref)

    hn = hn_ref[...]
    g = jnp.dot(hn, wg_ref[...], preferred_element_type=F32)
    up = jnp.dot(hn, wu_ref[...], preferred_element_type=F32)
    act = (g * jax.nn.sigmoid(g)) * up
    acc_ref[...] += jnp.dot(act.astype(BF16), wd_ref[...], preferred_element_type=F32)

    @pl.when(f == pl.num_programs(1) - 1)
    def _():
        o_ref[...] = h_ref[...] + _rms(acc_ref[...], gpost_ref[...])


def ffn(h, g_pre, w_gate_up, w_down, g_post, *, tm, tf):
    m, d = h.shape
    d_ff = w_down.shape[0]
    tm = min(tm, m)
    nf = d_ff // tf
    const = lambda i, f: (0, 0)
    return pl.pallas_call(
        _ffn_kernel,
        grid=(m // tm, nf),
        in_specs=[
            pl.BlockSpec((tm, d), lambda i, f: (i, 0)),
            pl.BlockSpec(g_pre.shape, const),
            pl.BlockSpec((d, tf), lambda i, f: (0, f)),
            pl.BlockSpec((d, tf), lambda i, f: (0, nf + f)),
            pl.BlockSpec((tf, d), lambda i, f: (f, 0)),
            pl.BlockSpec(g_post.shape, const),
        ],
        out_specs=pl.BlockSpec((tm, d), lambda i, f: (i, 0)),
        out_shape=jax.ShapeDtypeStruct((m, d), F32),
        scratch_shapes=[pltpu.VMEM((tm, d), BF16), pltpu.VMEM((tm, d), F32)],
        compiler_params=_params("parallel", "arbitrary"),
        name="ffn",
    )(h, g_pre, w_gate_up, w_gate_up, w_down, g_post)


def _rope_table(positions):
    half = MLA_ROPE_DIM // 2
    inv_freq = ROPE_THETA ** (-jnp.arange(half, dtype=F32) / half)
    ang = positions.astype(F32)[:, None] * inv_freq
    pad = jnp.zeros((positions.shape[0], LANES - MLA_ROPE_DIM), F32)
    return jnp.concatenate([jnp.cos(ang), jnp.sin(ang), pad], axis=1)


def _layer(l, x2, pos, norm_pre_mix, w_in, b_gate, q_norm, w_uq, kv_norm, w_ukv, w_o_sb, w_o_mla,
           w_out, norm_post_mix, norm_pre_ffn, w_gate_up, w_down, norm_post_ffn, *, batch, seq):
    d = x2.shape[1]
    sb_w = SB_HEADS * SB_HEAD_DIM
    o_qlat = 3 * sb_w
    o_kvlat = o_qlat + MLA_Q_RANK
    o_kr = o_kvlat + MLA_KV_RANK
    o_gate = o_kr + MLA_ROPE_DIM
    row = lambda v: v[l][None, :].astype(F32)

    w_in_l = w_in[l]
    w_gate_qkv = jnp.concatenate([w_in_l[:, o_gate:], w_in_l[:, :o_qlat]], axis=1).astype(BF16)
    w_lat = jnp.concatenate(
        [w_in_l[:, o_qlat:o_gate], jnp.zeros((d, LANES - MLA_ROPE_DIM), w_in_l.dtype)], axis=1).astype(BF16)

    qk_dim = MLA_NOPE_DIM + MLA_ROPE_DIM
    wq = w_uq[l].reshape(MLA_Q_RANK, MLA_HEADS, qk_dim)
    wq = jnp.pad(wq, ((0, 0), (0, 0), (0, MLA_QK_PAD - qk_dim)))
    wq = wq.reshape(MLA_Q_RANK, MLA_HEADS * MLA_QK_PAD).astype(BF16)
    wkv = w_ukv[l].reshape(MLA_KV_RANK, MLA_HEADS, MLA_NOPE_DIM + MLA_V_DIM)
    wkv = jnp.concatenate([wkv[:, :, :MLA_NOPE_DIM].reshape(MLA_KV_RANK, -1),
                           wkv[:, :, MLA_NOPE_DIM:].reshape(MLA_KV_RANK, -1)], axis=1).astype(BF16)

    g_pre = row(norm_pre_mix)
    sb_q_scale = -(SB_HEAD_DIM ** -0.5) * LOG2E
    n_gate = w_in_l.shape[1] - o_gate
    col_scale = jnp.concatenate([jnp.ones((1, n_gate), F32), jnp.full((1, sb_w), sb_q_scale, F32),
                                 jnp.ones((1, 2 * sb_w), F32)], axis=1)
    gate_qkv = norm_matmul(x2, g_pre, w_gate_qkv, col_scale, tm=1024, tn=1024)

    q_m, k_m, vt_m = mla_prep(x2, g_pre, w_lat, row(q_norm), wq, row(kv_norm), wkv, _rope_table(pos),
                              batch=batch, tm=MLA_KEY_BLOCK, q_scale=(qk_dim ** -0.5) * LOG2E)

    a_sb = sb_attention(gate_qkv, batch=batch, seq=seq, tq=1024, first_col=n_gate // SB_HEAD_DIM)
    a_mla = mla_attention(q_m, k_m, vt_m, batch=batch, seq=seq)

    h1 = merge(x2, a_sb, a_mla, gate_qkv, row(b_gate), w_o_sb[l].astype(BF16),
               w_o_mla[l].astype(BF16), w_out[l].astype(BF16), row(norm_post_mix), tm=256)
    return ffn(h1, row(norm_pre_ffn), w_gate_up[l].astype(BF16), w_down[l].astype(BF16),
               row(norm_post_ffn), tm=512, tf=512)


def kernel(x, positions, norm_pre_mix, w_in, b_gate, q_norm, w_uq, kv_norm, w_ukv, w_o_sb, w_o_mla,
           w_out, norm_post_mix, norm_pre_ffn, w_gate_up, w_down, norm_post_ffn):
    batch, seq, d = x.shape
    h = x.reshape(batch * seq, d)
    pos = positions.reshape(batch * seq)
    for l in range(w_in.shape[0]):
        h = _layer(l, h, pos, norm_pre_mix, w_in, b_gate, q_norm, w_uq, kv_norm, w_ukv, w_o_sb,
                   w_o_mla, w_out, norm_post_mix, norm_pre_ffn, w_gate_up, w_down, norm_post_ffn,
                   batch=batch, seq=seq)
    return h.reshape(batch, seq, d)
```

```python
import functools
import math

import jax
import jax.numpy as jnp
from jax import lax
from jax.experimental import pallas as pl
from jax.experimental.pallas import tpu as pltpu

F32 = jnp.float32
BF16 = jnp.bfloat16

CHUNK = 64
SB_HEADS = 8
SB_HEAD_DIM = 128
MLA_HEADS = 8
MLA_NOPE_DIM = 128
MLA_ROPE_DIM = 64
MLA_V_DIM = 128
MLA_Q_RANK = 768
MLA_KV_RANK = 512
ROPE_THETA = 10000.0
NORM_EPS = 1e-6
NEG_INF = -1e30
LOG2E = math.log2(math.e)

LANES = 128
MLA_QK_PAD = 2 * LANES
VMEM_LIMIT = 56 * 1024 * 1024
SB_BLOCK = 256
SB_DEAD_LOG2 = -160.0
BF16_ROWS = 16
MLA_KEY_BLOCK = 512


def _params(*semantics):
    return pltpu.CompilerParams(dimension_semantics=semantics, vmem_limit_bytes=VMEM_LIMIT)


def _rms(x, g):
    ms = jnp.mean(x * x, axis=-1, keepdims=True)
    return x * lax.rsqrt(ms + NORM_EPS) * g


def _norm_matmul_kernel(x_ref, g_ref, w_ref, cs_ref, o_ref, u_ref):
    @pl.when(pl.program_id(1) == 0)
    def _():
        u_ref[...] = _rms(x_ref[...], g_ref[...]).astype(BF16)

    acc = jnp.dot(u_ref[...], w_ref[...], preferred_element_type=F32)
    o_ref[...] = (acc * cs_ref[...]).astype(o_ref.dtype)


def norm_matmul(x, gain, w, col_scale, *, tm, tn):
    m, d = x.shape
    n = w.shape[1]
    tm = min(tm, m)
    return pl.pallas_call(
        _norm_matmul_kernel,
        grid=(m // tm, n // tn),
        in_specs=[
            pl.BlockSpec((tm, d), lambda i, j: (i, 0)),
            pl.BlockSpec((1, d), lambda i, j: (0, 0)),
            pl.BlockSpec((d, tn), lambda i, j: (0, j)),
            pl.BlockSpec((1, tn), lambda i, j: (0, j)),
        ],
        out_specs=pl.BlockSpec((tm, tn), lambda i, j: (i, j)),
        out_shape=jax.ShapeDtypeStruct((m, n), BF16),
        scratch_shapes=[pltpu.VMEM((tm, d), BF16)],
        compiler_params=_params("parallel", "arbitrary"),
        name="norm_matmul",
    )(x, gain, w, col_scale)


def _rope_coeffs(rope_ref):
    half = MLA_ROPE_DIM // 2
    t = rope_ref[...]
    lane = lax.broadcasted_iota(jnp.int32, t.shape, 1)
    first, second = lane < half, jnp.logical_and(lane >= half, lane < 2 * half)
    c = jnp.where(first, t, jnp.where(second, pltpu.roll(t, half, 1), 1.0))
    a = jnp.where(first, -pltpu.roll(t, LANES - half, 1), 0.0)
    b = jnp.where(second, t, 0.0)
    return c, a, b


def _rope128(r, coeffs):
    c, a, b = coeffs
    half = MLA_ROPE_DIM // 2
    return r * c + pltpu.roll(r, LANES - half, 1) * a + pltpu.roll(r, half, 1) * b


def _mla_prep_kernel(x_ref, g_ref, wlat_ref, qn_ref, wuq_ref, kvn_ref, wukv_ref, rope_ref,
                     q_ref, k_ref, vt_ref, *, q_scale):
    u = _rms(x_ref[...], g_ref[...]).astype(BF16)
    lat = jnp.dot(u, wlat_ref[...], preferred_element_type=F32)
    q_lat = lat[:, :MLA_Q_RANK]
    kv_lat = lat[:, MLA_Q_RANK:MLA_Q_RANK + MLA_KV_RANK]
    k_r = lat[:, MLA_Q_RANK + MLA_KV_RANK:]

    q = jnp.dot(_rms(q_lat, qn_ref[...]).astype(BF16), wuq_ref[...], preferred_element_type=F32)
    kv = jnp.dot(_rms(kv_lat, kvn_ref[...]).astype(BF16), wukv_ref[...], preferred_element_type=F32)
    coeffs = _rope_coeffs(rope_ref)
    k_rope = _rope128(k_r, coeffs).astype(BF16)
    ones = jnp.ones((BF16_ROWS, x_ref.shape[0]), BF16)

    kv_split = MLA_HEADS * MLA_NOPE_DIM
    for h in range(MLA_HEADS):
        lo = h * MLA_QK_PAD
        q_nope = q[:, lo:lo + LANES]
        q_rope = _rope128(q[:, lo + LANES:lo + 2 * LANES], coeffs)
        q_ref[:, lo:lo + LANES] = (q_nope * q_scale).astype(BF16)
        q_ref[:, lo + LANES:lo + 2 * LANES] = (q_rope * q_scale).astype(BF16)
        k_ref[:, lo:lo + LANES] = kv[:, h * LANES:(h + 1) * LANES].astype(BF16)
        k_ref[:, lo + LANES:lo + 2 * LANES] = k_rope
        v_h = kv[:, kv_split + h * MLA_V_DIM:kv_split + (h + 1) * MLA_V_DIM]
        vt_ref[h, :MLA_V_DIM, :] = v_h.T.astype(BF16)
        vt_ref[h, MLA_V_DIM:, :] = ones


def mla_prep(x, gain, w_lat, q_norm, w_uq, kv_norm, w_ukv, rope_tab, *, batch, tm, q_scale):
    m, d = x.shape
    n_tiles = m // batch // tm
    const = lambda i: (0, 0)
    qk_w = MLA_HEADS * MLA_QK_PAD
    d_ext = MLA_V_DIM + BF16_ROWS
    return pl.pallas_call(
        functools.partial(_mla_prep_kernel, q_scale=q_scale),
        grid=(m // tm,),
        in_specs=[
            pl.BlockSpec((tm, d), lambda i: (i, 0)),
            pl.BlockSpec(gain.shape, const),
            pl.BlockSpec(w_lat.shape, const),
            pl.BlockSpec(q_norm.shape, const),
            pl.BlockSpec(w_uq.shape, const),
            pl.BlockSpec(kv_norm.shape, const),
            pl.BlockSpec(w_ukv.shape, const),
            pl.BlockSpec((tm, LANES), lambda i: (i, 0)),
        ],
        out_specs=[
            pl.BlockSpec((tm, qk_w), lambda i: (i, 0)),
            pl.BlockSpec((tm, qk_w), lambda i: (i, 0)),
            pl.BlockSpec((None, MLA_HEADS, None, d_ext, tm), lambda i: (i // n_tiles, 0, i % n_tiles, 0, 0)),
        ],
        out_shape=[
            jax.ShapeDtypeStruct((m, qk_w), BF16),
            jax.ShapeDtypeStruct((m, qk_w), BF16),
            jax.ShapeDtypeStruct((batch, MLA_HEADS, n_tiles, d_ext, tm), BF16),
        ],
        compiler_params=_params("parallel"),
        name="mla_prep",
    )(x, gain, w_lat, q_norm, w_uq, kv_norm, w_ukv, rope_tab)


def _qkt(q, k):
    return lax.dot_general(q, k, (((1,), (1,)), ((), ())), preferred_element_type=F32)


def _sb_attn_kernel(q_ref, k_ref, v_ref, o_ref, run_ref, acc_ref, *, tq):
    tb = SB_BLOCK
    i = pl.program_id(2)
    subs = range(tq // tb)
    row = lax.broadcasted_iota(jnp.int32, (tb, tb), 0)
    col = lax.broadcasted_iota(jnp.int32, (tb, tb), 1)
    before = row > col
    later = before.astype(BF16)

    def log2_keep(nz):
        return jnp.minimum(nz, 0.0) - jnp.log(1.0 + jnp.exp2(-jnp.abs(nz))) * LOG2E

    def suffix(keep):
        return jnp.dot(keep.astype(BF16), later, preferred_element_type=F32)

    q_first = [i * tq + g * tb for g in subs]
    starts = [pl.multiple_of(jnp.maximum(q_first[g] - tb, 0), tb) for g in subs]
    nz = [_qkt(q_ref[g * tb:(g + 1) * tb, :], k_ref[pl.ds(starts[g], 2 * tb), :]) for g in subs]
    nz_lo = [x[:, :tb] for x in nz]
    nz_hi = [x[:, tb:] for x in nz]
    delta = q_first[0] - starts[0]
    valid_lo = [(col - row) < delta] + [None for _ in subs[1:]]
    valid_hi = [(col - row) < delta - tb] + [before for _ in subs[1:]]
    keep_lo = [log2_keep(x) for x in nz_lo]
    keep_lo[0] = jnp.where(valid_lo[0], keep_lo[0], 0.0)
    keep_hi = [jnp.where(valid_hi[g], log2_keep(nz_hi[g]), 0.0) for g in subs]
    suf_hi = [suffix(x) for x in keep_hi]
    suf_lo = [suffix(x) for x in keep_lo]
    run_lo = [suf_hi[g][:, 0:1] + keep_hi[g][:, 0:1] for g in subs]
    w_hi = [jnp.where(valid_hi[g], jnp.exp2((keep_hi[g] - nz_hi[g]) + suf_hi[g]), 0.0) for g in subs]
    w_lo = [jnp.exp2((keep_lo[g] - nz_lo[g]) + suf_lo[g] + run_lo[g]) for g in subs]
    w_lo[0] = jnp.where(valid_lo[0], w_lo[0], 0.0)
    for g in subs:
        w = jnp.concatenate([w_lo[g], w_hi[g]], axis=1).astype(BF16)
        acc_ref[g * tb:(g + 1) * tb, :] = jnp.dot(w, v_ref[pl.ds(starts[g], 2 * tb), :],
                                                  preferred_element_type=F32)
        run_ref[g * tb:(g + 1) * tb, :] = run_lo[g] + suf_lo[g][:, 0:1] + keep_lo[g][:, 0:1]

    def walk_back(g):
        rows = pl.ds(g * tb, tb)

        def alive(c):
            return jnp.logical_and(c[0] >= 0, c[1] > SB_DEAD_LOG2)

        def body(c):
            start = pl.multiple_of(c[0] * tb, tb)
            nz = _qkt(q_ref[rows, :], k_ref[pl.ds(start, tb), :])
            keep = log2_keep(nz)
            suf = suffix(keep)
            run = run_ref[rows, :]
            w = jnp.exp2((keep - nz) + suf + run)
            acc_ref[rows, :] += jnp.dot(w.astype(BF16), v_ref[pl.ds(start, tb), :],
                                        preferred_element_type=F32)
            run = run + suf[:, 0:1] + keep[:, 0:1]
            run_ref[rows, :] = run
            return c[0] - 1, jnp.max(run)

        lax.while_loop(alive, body, (i * len(subs) + g - 2, jnp.max(run_ref[rows, :])))

    @pl.when(jnp.max(run_ref[...]) > SB_DEAD_LOG2)
    def _():
        for g in subs:
            walk_back(g)

    o_ref[...] = acc_ref[...].astype(o_ref.dtype)


def sb_attention(qkv, *, batch, seq, tq, first_col):
    m = qkv.shape[0]
    d = SB_HEAD_DIM
    tq = min(tq, seq)
    assert tq % SB_BLOCK == 0 and seq >= 2 * SB_BLOCK
    nq = seq // tq
    return pl.pallas_call(
        functools.partial(_sb_attn_kernel, tq=tq),
        grid=(batch, SB_HEADS, nq),
        in_specs=[
            pl.BlockSpec((tq, d), lambda b, h, i: (b * nq + i, first_col + h)),
            pl.BlockSpec((seq, d), lambda b, h, i: (b, first_col + SB_HEADS + h)),
            pl.BlockSpec((seq, d), lambda b, h, i: (b, first_col + 2 * SB_HEADS + h)),
        ],
        out_specs=pl.BlockSpec((tq, d), lambda b, h, i: (b * nq + i, h)),
        out_shape=jax.ShapeDtypeStruct((m, SB_HEADS * d), BF16),
        scratch_shapes=[pltpu.VMEM((tq, 1), F32), pltpu.VMEM((tq, d), F32)],
        compiler_params=_params("parallel", "parallel", "arbitrary"),
        name="sb_attention",
    )(qkv, qkv, qkv)


def _mla_attn_kernel(q_ref, k_ref, vt_ref, o_ref, s_ref, smax_ref, acc_ref, *, tq, tk):
    i = pl.program_id(2)
    q = q_ref[...]
    assert tq == 2 * tk

    def scores(blk, slot):
        keys = pl.ds(pl.multiple_of(blk * tk, tk), tk)
        s = _qkt(k_ref[keys, :], q)
        s_ref[slot] = s
        smax_ref[slot] = jnp.max(s, axis=0, keepdims=True)

    def update(blk, slot, m_old, mask):
        s = s_ref[slot]
        if mask is None:
            s_max = smax_ref[slot]
        else:
            s = jnp.where(mask, s, NEG_INF)
            s_max = jnp.max(s, axis=0, keepdims=True)
        m_new = jnp.maximum(m_old, s_max)
        p = jnp.exp2(s - m_new).astype(BF16)
        pv = jnp.dot(vt_ref[blk], p, preferred_element_type=F32)
        acc_ref[...] = jnp.exp2(m_old - m_new) * acc_ref[...] + pv
        return m_new

    acc_ref[...] = jnp.zeros(acc_ref.shape, F32)
    m = jnp.full((1, tq), NEG_INF, F32)
    scores(0, 0)

    def pair(t, m):
        scores(2 * t + 1, 1)
        m = update(2 * t, 0, m, None)
        scores(2 * t + 2, 0)
        return update(2 * t + 1, 1, m, None)

    m = lax.fori_loop(0, i // 2, lambda u, m: pair(2 * u + 1, pair(2 * u, m)), m)
    m = lax.cond(i % 2 == 1, lambda m: pair(i - 1, m), lambda m: m, m)
    k_chunk = lax.broadcasted_iota(jnp.int32, (tk, tq), 0) // CHUNK
    q_chunk = lax.broadcasted_iota(jnp.int32, (tk, tq), 1) // CHUNK
    allowed = k_chunk <= q_chunk
    blk = 2 * i + 1
    s = _qkt(k_ref[pl.ds(pl.multiple_of(blk * tk, tk), tk), :], q_ref[tk:, :])
    m = update(2 * i, 0, m, allowed)
    s = jnp.where(allowed[:, :tk], s, NEG_INF)
    m_old = m[:, tk:]
    m_new = jnp.maximum(m_old, jnp.max(s, axis=0, keepdims=True))
    pv = jnp.dot(vt_ref[blk], jnp.exp2(s - m_new).astype(BF16), preferred_element_type=F32)
    acc_ref[:, tk:] = jnp.exp2(m_old - m_new) * acc_ref[:, tk:] + pv
    out_t = acc_ref[:MLA_V_DIM, :] / acc_ref[MLA_V_DIM:MLA_V_DIM + 1, :]
    o_ref[...] = out_t.T.astype(o_ref.dtype)


def mla_attention(q, k, vt, *, batch, seq):
    m = q.shape[0]
    _, _, nk, d_ext, tk = vt.shape
    tq = 2 * tk
    assert tk % CHUNK == 0 and nk * tk == seq
    nq = seq // tq
    return pl.pallas_call(
        functools.partial(_mla_attn_kernel, tq=tq, tk=tk),
        grid=(batch, MLA_HEADS, nq),
        in_specs=[
            pl.BlockSpec((tq, MLA_QK_PAD), lambda b, h, i: (b * nq + i, h)),
            pl.BlockSpec((seq, MLA_QK_PAD), lambda b, h, i: (b, h)),
            pl.BlockSpec((None, None, nk, d_ext, tk), lambda b, h, i: (b, h, 0, 0, 0)),
        ],
        out_specs=pl.BlockSpec((tq, MLA_V_DIM), lambda b, h, i: (b * nq + i, h)),
        out_shape=jax.ShapeDtypeStruct((m, MLA_HEADS * MLA_V_DIM), BF16),
        scratch_shapes=[pltpu.VMEM((2, tk, tq), F32), pltpu.VMEM((2, 1, tq), F32),
                        pltpu.VMEM((d_ext, tq), F32)],
        compiler_params=_params("parallel", "parallel", "arbitrary"),
        name="mla_attention",
    )(q, k, vt)


def _merge_kernel(x_ref, asb_ref, amla_ref, gl_ref, bg_ref, wosb_ref, womla_ref, wout_ref,
                  gpost_ref, o_ref):
    d = x_ref.shape[1]
    y_sb = jnp.dot(asb_ref[...], wosb_ref[...], preferred_element_type=F32)
    y_mla = jnp.dot(amla_ref[...], womla_ref[...], preferred_element_type=F32)
    gates = jax.nn.sigmoid(gl_ref[...].astype(F32) + bg_ref[...])
    mixed_in = gates[:, :d] * y_sb + gates[:, d:] * y_mla
    mixed = jnp.dot(mixed_in.astype(BF16), wout_ref[...], preferred_element_type=F32)
    o_ref[...] = x_ref[...] + _rms(mixed, gpost_ref[...])


def merge(x, a_sb, a_mla, gate_logits, b_gate, w_o_sb, w_o_mla, w_out, g_post, *, tm):
    m, d = x.shape
    tm = min(tm, m)
    const = lambda i: (0, 0)
    resident = dict(pipeline_mode=pl.Buffered(1))
    return pl.pallas_call(
        _merge_kernel,
        grid=(m // tm,),
        in_specs=[
            pl.BlockSpec((tm, d), lambda i: (i, 0)),
            pl.BlockSpec((tm, a_sb.shape[1]), lambda i: (i, 0)),
            pl.BlockSpec((tm, a_mla.shape[1]), lambda i: (i, 0)),
            pl.BlockSpec((tm, 2 * d), lambda i: (i, 0)),
            pl.BlockSpec(b_gate.shape, const),
            pl.BlockSpec(w_o_sb.shape, const, **resident),
            pl.BlockSpec(w_o_mla.shape, const, **resident),
            pl.BlockSpec(w_out.shape, const, **resident),
            pl.BlockSpec(g_post.shape, const),
        ],
        out_specs=pl.BlockSpec((tm, d), lambda i: (i, 0)),
        out_shape=jax.ShapeDtypeStruct((m, d), F32),
        compiler_params=_params("parallel"),
        name="merge",
    )(x, a_sb, a_mla, gate_logits, b_gate, w_o_sb, w_o_mla, w_out, g_post)


def _ffn_kernel(h_ref, gpre_ref, wg_ref, wu_ref, wd_ref, gpost_ref, o_ref, hn_ref, acc_ref):
    f = pl.program_id(1)

    @pl.when(f == 0)
    def _():
        hn_ref[...] = _rms(h_ref[...], gpre_ref[...]).astype(BF16)
        acc_ref[...] = jnp.zeros_like(acc_ref)

    hn = hn_ref[...]
    g = jnp.dot(hn, wg_ref[...], preferred_element_type=F32)
    up = jnp.dot(hn, wu_ref[...], preferred_element_type=F32)
    act = (g * jax.nn.sigmoid(g)) * up
    acc_ref[...] += jnp.dot(act.astype(BF16), wd_ref[...], preferred_element_type=F32)

    @pl.when(f == pl.num_programs(1) - 1)
    def _():
        o_ref[...] = h_ref[...] + _rms(acc_ref[...], gpost_ref[...])


def ffn(h, g_pre, w_gate_up, w_down, g_post, *, tm, tf):
    m, d = h.shape
    d_ff = w_down.shape[0]
    tm = min(tm, m)
    nf = d_ff // tf
    const = lambda i, f: (0, 0)
    return pl.pallas_call(
        _ffn_kernel,
        grid=(m // tm, nf),
        in_specs=[
            pl.BlockSpec((tm, d), lambda i, f: (i, 0)),
            pl.BlockSpec(g_pre.shape, const),
            pl.BlockSpec((d, tf), lambda i, f: (0, f)),
            pl.BlockSpec((d, tf), lambda i, f: (0, nf + f)),
            pl.BlockSpec((tf, d), lambda i, f: (f, 0)),
            pl.BlockSpec(g_post.shape, const),
        ],
        out_specs=pl.BlockSpec((tm, d), lambda i, f: (i, 0)),
        out_shape=jax.ShapeDtypeStruct((m, d), F32),
        scratch_shapes=[pltpu.VMEM((tm, d), BF16), pltpu.VMEM((tm, d), F32)],
        compiler_params=_params("parallel", "arbitrary"),
        name="ffn",
    )(h, g_pre, w_gate_up, w_gate_up, w_down, g_post)


def _rope_table(positions):
    half = MLA_ROPE_DIM // 2
    inv_freq = ROPE_THETA ** (-jnp.arange(half, dtype=F32) / half)
    ang = positions.astype(F32)[:, None] * inv_freq
    pad = jnp.zeros((positions.shape[0], LANES - MLA_ROPE_DIM), F32)
    return jnp.concatenate([jnp.cos(ang), jnp.sin(ang), pad], axis=1)


def _layer(l, x2, pos, norm_pre_mix, w_in, b_gate, q_norm, w_uq, kv_norm, w_ukv, w_o_sb, w_o_mla,
           w_out, norm_post_mix, norm_pre_ffn, w_gate_up, w_down, norm_post_ffn, *, batch, seq):
    d = x2.shape[1]
    sb_w = SB_HEADS * SB_HEAD_DIM
    o_qlat = 3 * sb_w
    o_kvlat = o_qlat + MLA_Q_RANK
    o_kr = o_kvlat + MLA_KV_RANK
    o_gate = o_kr + MLA_ROPE_DIM
    row = lambda v: v[l][None, :].astype(F32)

    w_in_l = w_in[l]
    w_gate_qkv = jnp.concatenate([w_in_l[:, o_gate:], w_in_l[:, :o_qlat]], axis=1).astype(BF16)
    w_lat = jnp.concatenate(
        [w_in_l[:, o_qlat:o_gate], jnp.zeros((d, LANES - MLA_ROPE_DIM), w_in_l.dtype)], axis=1).astype(BF16)

    qk_dim = MLA_NOPE_DIM + MLA_ROPE_DIM
    wq = w_uq[l].reshape(MLA_Q_RANK, MLA_HEADS, qk_dim)
    wq = jnp.pad(wq, ((0, 0), (0, 0), (0, MLA_QK_PAD - qk_dim)))
    wq = wq.reshape(MLA_Q_RANK, MLA_HEADS * MLA_QK_PAD).astype(BF16)
    wkv = w_ukv[l].reshape(MLA_KV_RANK, MLA_HEADS, MLA_NOPE_DIM + MLA_V_DIM)
    wkv = jnp.concatenate([wkv[:, :, :MLA_NOPE_DIM].reshape(MLA_KV_RANK, -1),
                           wkv[:, :, MLA_NOPE_DIM:].reshape(MLA_KV_RANK, -1)], axis=1).astype(BF16)

    g_pre = row(norm_pre_mix)
    sb_q_scale = -(SB_HEAD_DIM ** -0.5) * LOG2E
    n_gate = w_in_l.shape[1] - o_gate
    col_scale = jnp.concatenate([jnp.ones((1, n_gate), F32), jnp.full((1, sb_w), sb_q_scale, F32),
                                 jnp.ones((1, 2 * sb_w), F32)], axis=1)
    gate_qkv = norm_matmul(x2, g_pre, w_gate_qkv, col_scale, tm=1024, tn=1024)

    q_m, k_m, vt_m = mla_prep(x2, g_pre, w_lat, row(q_norm), wq, row(kv_norm), wkv, _rope_table(pos),
                              batch=batch, tm=MLA_KEY_BLOCK, q_scale=(qk_dim ** -0.5) * LOG2E)

    a_sb = sb_attention(gate_qkv, batch=batch, seq=seq, tq=1024, first_col=n_gate // SB_HEAD_DIM)
    a_mla = mla_attention(q_m, k_m, vt_m, batch=batch, seq=seq)

    h1 = merge(x2, a_sb, a_mla, gate_qkv, row(b_gate), w_o_sb[l].astype(BF16),
               w_o_mla[l].astype(BF16), w_out[l].astype(BF16), row(norm_post_mix), tm=256)
    return ffn(h1, row(norm_pre_ffn), w_gate_up[l].astype(BF16), w_down[l].astype(BF16),
               row(norm_post_ffn), tm=512, tf=512)


def kernel(x, positions, norm_pre_mix, w_in, b_gate, q_norm, w_uq, kv_norm, w_ukv, w_o_sb, w_o_mla,
           w_out, norm_post_mix, norm_pre_ffn, w_gate_up, w_down, norm_post_ffn):
    batch, seq, d = x.shape
    h = x.reshape(batch * seq, d)
    pos = positions.reshape(batch * seq)
    for l in range(w_in.shape[0]):
        h = _layer(l, h, pos, norm_pre_mix, w_in, b_gate, q_norm, w_uq, kv_norm, w_ukv, w_o_sb,
                   w_o_mla, w_out, norm_post_mix, norm_pre_ffn, w_gate_up, w_down, norm_post_ffn,
                   batch=batch, seq=seq)
    return h.reshape(batch, seq, d)
```

```python
import functools
import math

import jax
import jax.numpy as jnp
from jax import lax
from jax.experimental import pallas as pl
from jax.experimental.pallas import tpu as pltpu

F32 = jnp.float32
BF16 = jnp.bfloat16

CHUNK = 64
SB_HEADS = 8
SB_HEAD_DIM = 128
MLA_HEADS = 8
MLA_NOPE_DIM = 128
MLA_ROPE_DIM = 64
MLA_V_DIM = 128
MLA_Q_RANK = 768
MLA_KV_RANK = 512
ROPE_THETA = 10000.0
NORM_EPS = 1e-6
NEG_INF = -1e30
LOG2E = math.log2(math.e)

LANES = 128
MLA_QK_PAD = 2 * LANES
VMEM_LIMIT = 56 * 1024 * 1024
SB_BLOCK = 256
SB_DEAD_LOG2 = -160.0
BF16_ROWS = 16
MLA_KEY_BLOCK = 512


def _params(*semantics):
    return pltpu.CompilerParams(dimension_semantics=semantics, vmem_limit_bytes=VMEM_LIMIT)


def _rms(x, g):
    ms = jnp.mean(x * x, axis=-1, keepdims=True)
    return x * lax.rsqrt(ms + NORM_EPS) * g


def _norm_matmul_kernel(x_ref, g_ref, w_ref, cs_ref, o_ref, u_ref):
    @pl.when(pl.program_id(1) == 0)
    def _():
        u_ref[...] = _rms(x_ref[...], g_ref[...]).astype(BF16)

    acc = jnp.dot(u_ref[...], w_ref[...], preferred_element_type=F32)
    o_ref[...] = (acc * cs_ref[...]).astype(o_ref.dtype)


def norm_matmul(x, gain, w, col_scale, *, tm, tn):
    m, d = x.shape
    n = w.shape[1]
    tm = min(tm, m)
    return pl.pallas_call(
        _norm_matmul_kernel,
        grid=(m // tm, n // tn),
        in_specs=[
            pl.BlockSpec((tm, d), lambda i, j: (i, 0)),
            pl.BlockSpec((1, d), lambda i, j: (0, 0)),
            pl.BlockSpec((d, tn), lambda i, j: (0, j)),
            pl.BlockSpec((1, tn), lambda i, j: (0, j)),
        ],
        out_specs=pl.BlockSpec((tm, tn), lambda i, j: (i, j)),
        out_shape=jax.ShapeDtypeStruct((m, n), BF16),
        scratch_shapes=[pltpu.VMEM((tm, d), BF16)],
        compiler_params=_params("parallel", "arbitrary"),
        name="norm_matmul",
    )(x, gain, w, col_scale)


def _rope_coeffs(rope_ref):
    half = MLA_ROPE_DIM // 2
    t = rope_ref[...]
    lane = lax.broadcasted_iota(jnp.int32, t.shape, 1)
    first, second = lane < half, jnp.logical_and(lane >= half, lane < 2 * half)
    c = jnp.where(first, t, jnp.where(second, pltpu.roll(t, half, 1), 1.0))
    a = jnp.where(first, -pltpu.roll(t, LANES - half, 1), 0.0)
    b = jnp.where(second, t, 0.0)
    return c, a, b


def _rope128(r, coeffs):
    c, a, b = coeffs
    half = MLA_ROPE_DIM // 2
    return r * c + pltpu.roll(r, LANES - half, 1) * a + pltpu.roll(r, half, 1) * b


def _mla_prep_kernel(x_ref, g_ref, wlat_ref, qn_ref, wuq_ref, kvn_ref, wukv_ref, rope_ref,
                     qt_ref, k_ref, vt_ref, *, q_scale):
    u = _rms(x_ref[...], g_ref[...]).astype(BF16)
    lat = jnp.dot(u, wlat_ref[...], preferred_element_type=F32)
    q_lat = lat[:, :MLA_Q_RANK]
    kv_lat = lat[:, MLA_Q_RANK:MLA_Q_RANK + MLA_KV_RANK]
    k_r = lat[:, MLA_Q_RANK + MLA_KV_RANK:]

    q = jnp.dot(_rms(q_lat, qn_ref[...]).astype(BF16), wuq_ref[...], preferred_element_type=F32)
    kv = jnp.dot(_rms(kv_lat, kvn_ref[...]).astype(BF16), wukv_ref[...], preferred_element_type=F32)
    coeffs = _rope_coeffs(rope_ref)
    k_rope = _rope128(k_r, coeffs).astype(BF16)
    ones = jnp.ones((BF16_ROWS, x_ref.shape[0]), BF16)

    kv_split = MLA_HEADS * MLA_NOPE_DIM
    for h in range(MLA_HEADS):
        lo = h * MLA_QK_PAD
        q_nope = q[:, lo:lo + LANES]
        q_rope = _rope128(q[:, lo + LANES:lo + 2 * LANES], coeffs)
        qt_ref[h, :LANES, :] = (q_nope * q_scale).T.astype(BF16)
        qt_ref[h, LANES:, :] = (q_rope * q_scale).T.astype(BF16)
        k_ref[:, lo:lo + LANES] = kv[:, h * LANES:(h + 1) * LANES].astype(BF16)
        k_ref[:, lo + LANES:lo + 2 * LANES] = k_rope
        v_h = kv[:, kv_split + h * MLA_V_DIM:kv_split + (h + 1) * MLA_V_DIM]
        vt_ref[h, :MLA_V_DIM, :] = v_h.T.astype(BF16)
        vt_ref[h, MLA_V_DIM:, :] = ones


def mla_prep(x, gain, w_lat, q_norm, w_uq, kv_norm, w_ukv, rope_tab, *, batch, tm, q_scale):
    m, d = x.shape
    n_tiles = m // batch // tm
    const = lambda i: (0, 0)
    tile = lambda i: (i // n_tiles, 0, i % n_tiles, 0, 0)
    qk_w = MLA_HEADS * MLA_QK_PAD
    d_ext = MLA_V_DIM + BF16_ROWS
    return pl.pallas_call(
        functools.partial(_mla_prep_kernel, q_scale=q_scale),
        grid=(m // tm,),
        in_specs=[
            pl.BlockSpec((tm, d), lambda i: (i, 0)),
            pl.BlockSpec(gain.shape, const),
            pl.BlockSpec(w_lat.shape, const),
            pl.BlockSpec(q_norm.shape, const),
            pl.BlockSpec(w_uq.shape, const),
            pl.BlockSpec(kv_norm.shape, const),
            pl.BlockSpec(w_ukv.shape, const),
            pl.BlockSpec((tm, LANES), lambda i: (i, 0)),
        ],
        out_specs=[
            pl.BlockSpec((None, MLA_HEADS, None, MLA_QK_PAD, tm), tile),
            pl.BlockSpec((tm, qk_w), lambda i: (i, 0)),
            pl.BlockSpec((None, MLA_HEADS, None, d_ext, tm), tile),
        ],
        out_shape=[
            jax.ShapeDtypeStruct((batch, MLA_HEADS, n_tiles, MLA_QK_PAD, tm), BF16),
            jax.ShapeDtypeStruct((m, qk_w), BF16),
            jax.ShapeDtypeStruct((batch, MLA_HEADS, n_tiles, d_ext, tm), BF16),
        ],
        compiler_params=_params("parallel"),
        name="mla_prep",
    )(x, gain, w_lat, q_norm, w_uq, kv_norm, w_ukv, rope_tab)


def _qkt(q, k):
    return lax.dot_general(q, k, (((1,), (1,)), ((), ())), preferred_element_type=F32)


def _sb_attn_kernel(q_ref, k_ref, v_ref, o_ref, run_ref, acc_ref, *, tq):
    tb = SB_BLOCK
    i = pl.program_id(2)
    subs = range(tq // tb)
    row = lax.broadcasted_iota(jnp.int32, (tb, tb), 0)
    col = lax.broadcasted_iota(jnp.int32, (tb, tb), 1)
    before = row > col
    later = before.astype(BF16)

    def log2_keep(nz):
        return jnp.minimum(nz, 0.0) - jnp.log(1.0 + jnp.exp2(-jnp.abs(nz))) * LOG2E

    def suffix(keep):
        return jnp.dot(keep.astype(BF16), later, preferred_element_type=F32)

    q_first = [i * tq + g * tb for g in subs]
    starts = [pl.multiple_of(jnp.maximum(q_first[g] - tb, 0), tb) for g in subs]
    nz = [_qkt(q_ref[g * tb:(g + 1) * tb, :], k_ref[pl.ds(starts[g], 2 * tb), :]) for g in subs]
    nz_lo = [x[:, :tb] for x in nz]
    nz_hi = [x[:, tb:] for x in nz]
    delta = q_first[0] - starts[0]
    valid_lo = [(col - row) < delta] + [None for _ in subs[1:]]
    valid_hi = [(col - row) < delta - tb] + [before for _ in subs[1:]]
    keep_lo = [log2_keep(x) for x in nz_lo]
    keep_lo[0] = jnp.where(valid_lo[0], keep_lo[0], 0.0)
    keep_hi = [jnp.where(valid_hi[g], log2_keep(nz_hi[g]), 0.0) for g in subs]
    suf_hi = [suffix(x) for x in keep_hi]
    suf_lo = [suffix(x) for x in keep_lo]
    run_lo = [suf_hi[g][:, 0:1] + keep_hi[g][:, 0:1] for g in subs]
    w_hi = [jnp.where(valid_hi[g], jnp.exp2((keep_hi[g] - nz_hi[g]) + suf_hi[g]), 0.0) for g in subs]
    w_lo = [jnp.exp2((keep_lo[g] - nz_lo[g]) + suf_lo[g] + run_lo[g]) for g in subs]
    w_lo[0] = jnp.where(valid_lo[0], w_lo[0], 0.0)
    for g in subs:
        w = jnp.concatenate([w_lo[g], w_hi[g]], axis=1).astype(BF16)
        acc_ref[g * tb:(g + 1) * tb, :] = jnp.dot(w, v_ref[pl.ds(starts[g], 2 * tb), :],
                                                  preferred_element_type=F32)
        run_ref[g * tb:(g + 1) * tb, :] = run_lo[g] + suf_lo[g][:, 0:1] + keep_lo[g][:, 0:1]

    def walk_back(g):
        rows = pl.ds(g * tb, tb)

        def alive(c):
            return jnp.logical_and(c[0] >= 0, c[1] > SB_DEAD_LOG2)

        def body(c):
            start = pl.multiple_of(c[0] * tb, tb)
            nz = _qkt(q_ref[rows, :], k_ref[pl.ds(start, tb), :])
            keep = log2_keep(nz)
            suf = suffix(keep)
            run = run_ref[rows, :]
            w = jnp.exp2((keep - nz) + suf + run)
            acc_ref[rows, :] += jnp.dot(w.astype(BF16), v_ref[pl.ds(start, tb), :],
                                        preferred_element_type=F32)
            run = run + suf[:, 0:1] + keep[:, 0:1]
            run_ref[rows, :] = run
            return c[0] - 1, jnp.max(run)

        lax.while_loop(alive, body, (i * len(subs) + g - 2, jnp.max(run_ref[rows, :])))

    @pl.when(jnp.max(run_ref[...]) > SB_DEAD_LOG2)
    def _():
        for g in subs:
            walk_back(g)

    o_ref[...] = acc_ref[...].astype(o_ref.dtype)


def sb_attention(qkv, *, batch, seq, tq, first_col):
    m = qkv.shape[0]
    d = SB_HEAD_DIM
    tq = min(tq, seq)
    assert tq % SB_BLOCK == 0 and seq >= 2 * SB_BLOCK
    nq = seq // tq
    return pl.pallas_call(
        functools.partial(_sb_attn_kernel, tq=tq),
        grid=(batch, SB_HEADS, nq),
        in_specs=[
            pl.BlockSpec((tq, d), lambda b, h, i: (b * nq + i, first_col + h)),
            pl.BlockSpec((seq, d), lambda b, h, i: (b, first_col + SB_HEADS + h)),
            pl.BlockSpec((seq, d), lambda b, h, i: (b, first_col + 2 * SB_HEADS + h)),
        ],
        out_specs=pl.BlockSpec((tq, d), lambda b, h, i: (b * nq + i, h)),
        out_shape=jax.ShapeDtypeStruct((m, SB_HEADS * d), BF16),
        scratch_shapes=[pltpu.VMEM((tq, 1), F32), pltpu.VMEM((tq, d), F32)],
        compiler_params=_params("parallel", "parallel", "arbitrary"),
        name="sb_attention",
    )(qkv, qkv, qkv)


def _mla_attn_kernel(qt_ref, k_ref, vt_ref, o_ref, s_ref, smax_ref, acc_ref, *, tq, tk):
    i = pl.program_id(2)
    assert tq == 2 * tk

    def scores(blk, slot):
        k_blk = k_ref[pl.ds(pl.multiple_of(blk * tk, tk), tk), :]
        for half in range(2):
            s = jnp.dot(k_blk, qt_ref[half], preferred_element_type=F32)
            s_ref[slot, :, half * tk:(half + 1) * tk] = s
            smax_ref[slot, :, half * tk:(half + 1) * tk] = jnp.max(s, axis=0, keepdims=True)

    def update(blk, slot, m_old, mask):
        s = s_ref[slot]
        if mask is None:
            s_max = smax_ref[slot]
        else:
            s = jnp.where(mask, s, NEG_INF)
            s_max = jnp.max(s, axis=0, keepdims=True)
        m_new = jnp.maximum(m_old, s_max)
        p = jnp.exp2(s - m_new).astype(BF16)
        pv = jnp.dot(vt_ref[blk], p, preferred_element_type=F32)
        acc_ref[...] = jnp.exp2(m_old - m_new) * acc_ref[...] + pv
        return m_new

    acc_ref[...] = jnp.zeros(acc_ref.shape, F32)
    m = jnp.full((1, tq), NEG_INF, F32)
    scores(0, 0)

    def pair(t, m):
        scores(2 * t + 1, 1)
        m = update(2 * t, 0, m, None)
        scores(2 * t + 2, 0)
        return update(2 * t + 1, 1, m, None)

    m = lax.fori_loop(0, i // 2, lambda u, m: pair(2 * u + 1, pair(2 * u, m)), m)
    m = lax.cond(i % 2 == 1, lambda m: pair(i - 1, m), lambda m: m, m)
    k_chunk = lax.broadcasted_iota(jnp.int32, (tk, tq), 0) // CHUNK
    q_chunk = lax.broadcasted_iota(jnp.int32, (tk, tq), 1) // CHUNK
    allowed = k_chunk <= q_chunk
    blk = 2 * i + 1
    s = jnp.dot(k_ref[pl.ds(pl.multiple_of(blk * tk, tk), tk), :], qt_ref[1],
                preferred_element_type=F32)
    m = update(2 * i, 0, m, allowed)
    s = jnp.where(allowed[:, :tk], s, NEG_INF)
    m_old = m[:, tk:]
    m_new = jnp.maximum(m_old, jnp.max(s, axis=0, keepdims=True))
    pv = jnp.dot(vt_ref[blk], jnp.exp2(s - m_new).astype(BF16), preferred_element_type=F32)
    acc_ref[:, tk:] = jnp.exp2(m_old - m_new) * acc_ref[:, tk:] + pv
    out_t = acc_ref[:MLA_V_DIM, :] / acc_ref[MLA_V_DIM:MLA_V_DIM + 1, :]
    o_ref[...] = out_t.T.astype(o_ref.dtype)


def mla_attention(qt, k, vt, *, batch, seq):
    m = k.shape[0]
    _, _, nk, d_ext, tk = vt.shape
    tq = 2 * tk
    assert tk % CHUNK == 0 and nk * tk == seq
    nq = seq // tq
    return pl.pallas_call(
        functools.partial(_mla_attn_kernel, tq=tq, tk=tk),
        grid=(batch, MLA_HEADS, nq),
        in_specs=[
            pl.BlockSpec((None, None, 2, MLA_QK_PAD, tk), lambda b, h, i: (b, h, i, 0, 0)),
            pl.BlockSpec((seq, MLA_QK_PAD), lambda b, h, i: (b, h)),
            pl.BlockSpec((None, None, nk, d_ext, tk), lambda b, h, i: (b, h, 0, 0, 0)),
        ],
        out_specs=pl.BlockSpec((tq, MLA_V_DIM), lambda b, h, i: (b * nq + i, h)),
        out_shape=jax.ShapeDtypeStruct((m, MLA_HEADS * MLA_V_DIM), BF16),
        scratch_shapes=[pltpu.VMEM((2, tk, tq), F32), pltpu.VMEM((2, 1, tq), F32),
                        pltpu.VMEM((d_ext, tq), F32)],
        compiler_params=_params("parallel", "parallel", "arbitrary"),
        name="mla_attention",
    )(qt, k, vt)


def _merge_kernel(x_ref, asb_ref, amla_ref, gl_ref, bg_ref, wosb_ref, womla_ref, wout_ref,
                  gpost_ref, o_ref):
    d = x_ref.shape[1]
    y_sb = jnp.dot(asb_ref[...], wosb_ref[...], preferred_element_type=F32)
    y_mla = jnp.dot(amla_ref[...], womla_ref[...], preferred_element_type=F32)
    gates = jax.nn.sigmoid(gl_ref[...].astype(F32) + bg_ref[...])
    mixed_in = gates[:, :d] * y_sb + gates[:, d:] * y_mla
    mixed = jnp.dot(mixed_in.astype(BF16), wout_ref[...], preferred_element_type=F32)
    o_ref[...] = x_ref[...] + _rms(mixed, gpost_ref[...])


def merge(x, a_sb, a_mla, gate_logits, b_gate, w_o_sb, w_o_mla, w_out, g_post, *, tm):
    m, d = x.shape
    tm = min(tm, m)
    const = lambda i: (0, 0)
    resident = dict(pipeline_mode=pl.Buffered(1))
    return pl.pallas_call(
        _merge_kernel,
        grid=(m // tm,),
        in_specs=[
            pl.BlockSpec((tm, d), lambda i: (i, 0)),
            pl.BlockSpec((tm, a_sb.shape[1]), lambda i: (i, 0)),
            pl.BlockSpec((tm, a_mla.shape[1]), lambda i: (i, 0)),
            pl.BlockSpec((tm, 2 * d), lambda i: (i, 0)),
            pl.BlockSpec(b_gate.shape, const),
            pl.BlockSpec(w_o_sb.shape, const, **resident),
            pl.BlockSpec(w_o_mla.shape, const, **resident),
            pl.BlockSpec(w_out.shape, const, **resident),
            pl.BlockSpec(g_post.shape, const),
        ],
        out_specs=pl.BlockSpec((tm, d), lambda i: (i, 0)),
        out_shape=jax.ShapeDtypeStruct((m, d), F32),
        compiler_params=_params("parallel"),
        name="merge",
    )(x, a_sb, a_mla, gate_logits, b_gate, w_o_sb, w_o_mla, w_out, g_post)


def _ffn_kernel(h_ref, gpre_ref, wg_ref, wu_ref, wd_ref, gpost_ref, o_ref, hn_ref, acc_ref):
    f = pl.program_id(1)

    @pl.when(f == 0)
    def _():
        hn_ref[...] = _rms(h_ref[...], gpre_ref[...]).astype(BF16)
        acc_ref[...] = jnp.zeros_like(acc_ref)

    hn = hn_ref[...]
    g = jnp.dot(hn, wg_ref[...], preferred_element_type=F32)
    up = jnp.dot(hn, wu_ref[...], preferred_element_type=F32)
    act = (g * jax.nn.sigmoid(g)) * up
    acc_ref[...] += jnp.dot(act.astype(BF16), wd_ref[...], preferred_element_type=F32)

    @pl.when(f == pl.num_programs(1) - 1)
    def _():
        o_ref[...] = h_ref[...] + _rms(acc_ref[...], gpost_ref[...])


def ffn(h, g_pre, w_gate_up, w_down, g_post, *, tm, tf):
    m, d = h.shape
    d_ff = w_down.shape[0]
    tm = min(tm, m)
    nf = d_ff // tf
    const = lambda i, f: (0, 0)
    return pl.pallas_call(
        _ffn_kernel,
        grid=(m // tm, nf),
        in_specs=[
            pl.BlockSpec((tm, d), lambda i, f: (i, 0)),
            pl.BlockSpec(g_pre.shape, const),
            pl.BlockSpec((d, tf), lambda i, f: (0, f)),
            pl.BlockSpec((d, tf), lambda i, f: (0, nf + f)),
            pl.BlockSpec((tf, d), lambda i, f: (f, 0)),
            pl.BlockSpec(g_post.shape, const),
        ],
        out_specs=pl.BlockSpec((tm, d), lambda i, f: (i, 0)),
        out_shape=jax.ShapeDtypeStruct((m, d), F32),
        scratch_shapes=[pltpu.VMEM((tm, d), BF16), pltpu.VMEM((tm, d), F32)],
        compiler_params=_params("parallel", "arbitrary"),
        name="ffn",
    )(h, g_pre, w_gate_up, w_gate_up, w_down, g_post)


def _rope_table(positions):
    half = MLA_ROPE_DIM // 2
    inv_freq = ROPE_THETA ** (-jnp.arange(half, dtype=F32) / half)
    ang = positions.astype(F32)[:, None] * inv_freq
    pad = jnp.zeros((positions.shape[0], LANES - MLA_ROPE_DIM), F32)
    return jnp.concatenate([jnp.cos(ang), jnp.sin(ang), pad], axis=1)


def _layer(l, x2, pos, norm_pre_mix, w_in, b_gate, q_norm, w_uq, kv_norm, w_ukv, w_o_sb, w_o_mla,
           w_out, norm_post_mix, norm_pre_ffn, w_gate_up, w_down, norm_post_ffn, *, batch, seq):
    d = x2.shape[1]
    sb_w = SB_HEADS * SB_HEAD_DIM
    o_qlat = 3 * sb_w
    o_kvlat = o_qlat + MLA_Q_RANK
    o_kr = o_kvlat + MLA_KV_RANK
    o_gate = o_kr + MLA_ROPE_DIM
    row = lambda v: v[l][None, :].astype(F32)

    w_in_l = w_in[l]
    w_gate_qkv = jnp.concatenate([w_in_l[:, o_gate:], w_in_l[:, :o_qlat]], axis=1).astype(BF16)
    w_lat = jnp.concatenate(
        [w_in_l[:, o_qlat:o_gate], jnp.zeros((d, LANES - MLA_ROPE_DIM), w_in_l.dtype)], axis=1).astype(BF16)

    qk_dim = MLA_NOPE_DIM + MLA_ROPE_DIM
    wq = w_uq[l].reshape(MLA_Q_RANK, MLA_HEADS, qk_dim)
    wq = jnp.pad(wq, ((0, 0), (0, 0), (0, MLA_QK_PAD - qk_dim)))
    wq = wq.reshape(MLA_Q_RANK, MLA_HEADS * MLA_QK_PAD).astype(BF16)
    wkv = w_ukv[l].reshape(MLA_KV_RANK, MLA_HEADS, MLA_NOPE_DIM + MLA_V_DIM)
    wkv = jnp.concatenate([wkv[:, :, :MLA_NOPE_DIM].reshape(MLA_KV_RANK, -1),
                           wkv[:, :, MLA_NOPE_DIM:].reshape(MLA_KV_RANK, -1)], axis=1).astype(BF16)

    g_pre = row(norm_pre_mix)
    sb_q_scale = -(SB_HEAD_DIM ** -0.5) * LOG2E
    n_gate = w_in_l.shape[1] - o_gate
    col_scale = jnp.concatenate([jnp.ones((1, n_gate), F32), jnp.full((1, sb_w), sb_q_scale, F32),
                                 jnp.ones((1, 2 * sb_w), F32)], axis=1)
    gate_qkv = norm_matmul(x2, g_pre, w_gate_qkv, col_scale, tm=1024, tn=1024)

    q_m, k_m, vt_m = mla_prep(x2, g_pre, w_lat, row(q_norm), wq, row(kv_norm), wkv, _rope_table(pos),
                              batch=batch, tm=MLA_KEY_BLOCK, q_scale=(qk_dim ** -0.5) * LOG2E)

    a_sb = sb_attention(gate_qkv, batch=batch, seq=seq, tq=1024, first_col=n_gate // SB_HEAD_DIM)
    a_mla = mla_attention(q_m, k_m, vt_m, batch=batch, seq=seq)

    h1 = merge(x2, a_sb, a_mla, gate_qkv, row(b_gate), w_o_sb[l].astype(BF16),
               w_o_mla[l].astype(BF16), w_out[l].astype(BF16), row(norm_post_mix), tm=256)
    return ffn(h1, row(norm_pre_ffn), w_gate_up[l].astype(BF16), w_down[l].astype(BF16),
               row(norm_post_ffn), tm=512, tf=512)


def kernel(x, positions, norm_pre_mix, w_in, b_gate, q_norm, w_uq, kv_norm, w_ukv, w_o_sb, w_o_mla,
           w_out, norm_post_mix, norm_pre_ffn, w_gate_up, w_down, norm_post_ffn):
    batch, seq, d = x.shape
    h = x.reshape(batch * seq, d)
    pos = positions.reshape(batch * seq)
    for l in range(w_in.shape[0]):
        h = _layer(l, h, pos, norm_pre_mix, w_in, b_gate, q_norm, w_uq, kv_norm, w_ukv, w_o_sb,
                   w_o_mla, w_out, norm_post_mix, norm_pre_ffn, w_gate_up, w_down, norm_post_ffn,
                   batch=batch, seq=seq)
    return h.reshape(batch, seq, d)
```

```python
import functools
import math

import jax
import jax.numpy as jnp
from jax import lax
from jax.experimental import pallas as pl
from jax.experimental.pallas import tpu as pltpu

F32 = jnp.float32
BF16 = jnp.bfloat16

CHUNK = 64
SB_HEADS = 8
SB_HEAD_DIM = 128
MLA_HEADS = 8
MLA_NOPE_DIM = 128
MLA_ROPE_DIM = 64
MLA_V_DIM = 128
MLA_Q_RANK = 768
MLA_KV_RANK = 512
ROPE_THETA = 10000.0
NORM_EPS = 1e-6
NEG_INF = -1e30
LOG2E = math.log2(math.e)

LANES = 128
MLA_QK_PAD = 2 * LANES
VMEM_LIMIT = 56 * 1024 * 1024
SB_BLOCK = 256
SB_DEAD_LOG2 = -160.0
BF16_ROWS = 16
MLA_KEY_BLOCK = 512
IN_PROJ_TM, IN_PROJ_TN = 1024, 1024
SB_QUERY_BLOCK = 8 * SB_BLOCK
MERGE_TM = 256
FFN_TM, FFN_TF = 512, 512


def _params(*semantics):
    return pltpu.CompilerParams(dimension_semantics=semantics, vmem_limit_bytes=VMEM_LIMIT)


def _rms(x, g):
    ms = jnp.mean(x * x, axis=-1, keepdims=True)
    return x * lax.rsqrt(ms + NORM_EPS) * g


def _norm_matmul_kernel(x_ref, g_ref, w_ref, cs_ref, o_ref, u_ref):
    @pl.when(pl.program_id(1) == 0)
    def _():
        u_ref[...] = _rms(x_ref[...], g_ref[...]).astype(BF16)

    acc = jnp.dot(u_ref[...], w_ref[...], preferred_element_type=F32)
    o_ref[...] = (acc * cs_ref[...]).astype(o_ref.dtype)


def norm_matmul(x, gain, w, col_scale, *, tm, tn):
    m, d = x.shape
    n = w.shape[1]
    tm = min(tm, m)
    return pl.pallas_call(
        _norm_matmul_kernel,
        grid=(m // tm, n // tn),
        in_specs=[
            pl.BlockSpec((tm, d), lambda i, j: (i, 0)),
            pl.BlockSpec((1, d), lambda i, j: (0, 0)),
            pl.BlockSpec((d, tn), lambda i, j: (0, j)),
            pl.BlockSpec((1, tn), lambda i, j: (0, j)),
        ],
        out_specs=pl.BlockSpec((tm, tn), lambda i, j: (i, j)),
        out_shape=jax.ShapeDtypeStruct((m, n), BF16),
        scratch_shapes=[pltpu.VMEM((tm, d), BF16)],
        compiler_params=_params("parallel", "arbitrary"),
        name="norm_matmul",
    )(x, gain, w, col_scale)


def _rope_coeffs(rope_ref):
    half = MLA_ROPE_DIM // 2
    t = rope_ref[...]
    lane = lax.broadcasted_iota(jnp.int32, t.shape, 1)
    first, second = lane < half, jnp.logical_and(lane >= half, lane < 2 * half)
    c = jnp.where(first, t, jnp.where(second, pltpu.roll(t, half, 1), 1.0))
    a = jnp.where(first, -pltpu.roll(t, LANES - half, 1), 0.0)
    b = jnp.where(second, t, 0.0)
    return c, a, b


def _rope128(r, coeffs):
    c, a, b = coeffs
    half = MLA_ROPE_DIM // 2
    return r * c + pltpu.roll(r, LANES - half, 1) * a + pltpu.roll(r, half, 1) * b


def _mla_prep_kernel(x_ref, g_ref, wlat_ref, qn_ref, wuq_ref, kvn_ref, wukv_ref, rope_ref,
                     qt_ref, k_ref, vt_ref, *, q_scale):
    u = _rms(x_ref[...], g_ref[...]).astype(BF16)
    lat = jnp.dot(u, wlat_ref[...], preferred_element_type=F32)
    q_lat = lat[:, :MLA_Q_RANK]
    kv_lat = lat[:, MLA_Q_RANK:MLA_Q_RANK + MLA_KV_RANK]
    k_r = lat[:, MLA_Q_RANK + MLA_KV_RANK:]

    q = jnp.dot(_rms(q_lat, qn_ref[...]).astype(BF16), wuq_ref[...], preferred_element_type=F32)
    kv = jnp.dot(_rms(kv_lat, kvn_ref[...]).astype(BF16), wukv_ref[...], preferred_element_type=F32)
    coeffs = _rope_coeffs(rope_ref)
    k_rope = _rope128(k_r, coeffs).astype(BF16)
    ones = jnp.ones((BF16_ROWS, x_ref.shape[0]), BF16)

    kv_split = MLA_HEADS * MLA_NOPE_DIM
    for h in range(MLA_HEADS):
        lo = h * MLA_QK_PAD
        q_nope = q[:, lo:lo + LANES]
        q_rope = _rope128(q[:, lo + LANES:lo + 2 * LANES], coeffs)
        qt_ref[h, :LANES, :] = (q_nope * q_scale).T.astype(BF16)
        qt_ref[h, LANES:, :] = (q_rope * q_scale).T.astype(BF16)
        k_ref[:, lo:lo + LANES] = kv[:, h * LANES:(h + 1) * LANES].astype(BF16)
        k_ref[:, lo + LANES:lo + 2 * LANES] = k_rope
        v_h = kv[:, kv_split + h * MLA_V_DIM:kv_split + (h + 1) * MLA_V_DIM]
        vt_ref[h, :MLA_V_DIM, :] = v_h.T.astype(BF16)
        vt_ref[h, MLA_V_DIM:, :] = ones


def mla_prep(x, gain, w_lat, q_norm, w_uq, kv_norm, w_ukv, rope_tab, *, batch, tm, q_scale):
    m, d = x.shape
    n_tiles = m // batch // tm
    const = lambda i: (0, 0)
    tile = lambda i: (i // n_tiles, 0, i % n_tiles, 0, 0)
    qk_w = MLA_HEADS * MLA_QK_PAD
    d_ext = MLA_V_DIM + BF16_ROWS
    return pl.pallas_call(
        functools.partial(_mla_prep_kernel, q_scale=q_scale),
        grid=(m // tm,),
        in_specs=[
            pl.BlockSpec((tm, d), lambda i: (i, 0)),
            pl.BlockSpec(gain.shape, const),
            pl.BlockSpec(w_lat.shape, const),
            pl.BlockSpec(q_norm.shape, const),
            pl.BlockSpec(w_uq.shape, const),
            pl.BlockSpec(kv_norm.shape, const),
            pl.BlockSpec(w_ukv.shape, const),
            pl.BlockSpec((tm, LANES), lambda i: (i, 0)),
        ],
        out_specs=[
            pl.BlockSpec((None, MLA_HEADS, None, MLA_QK_PAD, tm), tile),
            pl.BlockSpec((tm, qk_w), lambda i: (i, 0)),
            pl.BlockSpec((None, MLA_HEADS, None, d_ext, tm), tile),
        ],
        out_shape=[
            jax.ShapeDtypeStruct((batch, MLA_HEADS, n_tiles, MLA_QK_PAD, tm), BF16),
            jax.ShapeDtypeStruct((m, qk_w), BF16),
            jax.ShapeDtypeStruct((batch, MLA_HEADS, n_tiles, d_ext, tm), BF16),
        ],
        compiler_params=_params("parallel"),
        name="mla_prep",
    )(x, gain, w_lat, q_norm, w_uq, kv_norm, w_ukv, rope_tab)


def _qkt(q, k):
    return lax.dot_general(q, k, (((1,), (1,)), ((), ())), preferred_element_type=F32)


def _sb_attn_kernel(q_ref, k_ref, v_ref, o_ref, run_ref, acc_ref, *, tq):
    tb = SB_BLOCK
    i = pl.program_id(2)
    subs = range(tq // tb)
    row = lax.broadcasted_iota(jnp.int32, (tb, tb), 0)
    col = lax.broadcasted_iota(jnp.int32, (tb, tb), 1)
    before = row > col
    later = before.astype(BF16)

    def log2_keep(nz):
        return jnp.minimum(nz, 0.0) - jnp.log(1.0 + jnp.exp2(-jnp.abs(nz))) * LOG2E

    def suffix(keep):
        return jnp.dot(keep.astype(BF16), later, preferred_element_type=F32)

    q_first = [i * tq + g * tb for g in subs]
    starts = [pl.multiple_of(jnp.maximum(q_first[g] - tb, 0), tb) for g in subs]
    nz = [_qkt(q_ref[g * tb:(g + 1) * tb, :], k_ref[pl.ds(starts[g], 2 * tb), :]) for g in subs]
    nz_lo = [x[:, :tb] for x in nz]
    nz_hi = [x[:, tb:] for x in nz]
    delta = q_first[0] - starts[0]
    valid_lo = [(col - row) < delta] + [None for _ in subs[1:]]
    valid_hi = [(col - row) < delta - tb] + [before for _ in subs[1:]]
    keep_lo = [log2_keep(x) for x in nz_lo]
    keep_lo[0] = jnp.where(valid_lo[0], keep_lo[0], 0.0)
    keep_hi = [jnp.where(valid_hi[g], log2_keep(nz_hi[g]), 0.0) for g in subs]
    suf_hi = [suffix(x) for x in keep_hi]
    suf_lo = [suffix(x) for x in keep_lo]
    run_lo = [suf_hi[g][:, 0:1] + keep_hi[g][:, 0:1] for g in subs]
    w_hi = [jnp.where(valid_hi[g], jnp.exp2((keep_hi[g] - nz_hi[g]) + suf_hi[g]), 0.0) for g in subs]
    w_lo = [jnp.exp2((keep_lo[g] - nz_lo[g]) + suf_lo[g] + run_lo[g]) for g in subs]
    w_lo[0] = jnp.where(valid_lo[0], w_lo[0], 0.0)
    for g in subs:
        w = jnp.concatenate([w_lo[g], w_hi[g]], axis=1).astype(BF16)
        acc_ref[g * tb:(g + 1) * tb, :] = jnp.dot(w, v_ref[pl.ds(starts[g], 2 * tb), :],
                                                  preferred_element_type=F32)
        run_ref[g * tb:(g + 1) * tb, :] = run_lo[g] + suf_lo[g][:, 0:1] + keep_lo[g][:, 0:1]

    def walk_back(g):
        rows = pl.ds(g * tb, tb)

        def alive(c):
            return jnp.logical_and(c[0] >= 0, c[1] > SB_DEAD_LOG2)

        def body(c):
            start = pl.multiple_of(c[0] * tb, tb)
            nz = _qkt(q_ref[rows, :], k_ref[pl.ds(start, tb), :])
            keep = log2_keep(nz)
            suf = suffix(keep)
            run = run_ref[rows, :]
            w = jnp.exp2((keep - nz) + suf + run)
            acc_ref[rows, :] += jnp.dot(w.astype(BF16), v_ref[pl.ds(start, tb), :],
                                        preferred_element_type=F32)
            run = run + suf[:, 0:1] + keep[:, 0:1]
            run_ref[rows, :] = run
            return c[0] - 1, jnp.max(run)

        lax.while_loop(alive, body, (i * len(subs) + g - 2, jnp.max(run_ref[rows, :])))

    @pl.when(jnp.max(run_ref[...]) > SB_DEAD_LOG2)
    def _():
        for g in subs:
            walk_back(g)

    o_ref[...] = acc_ref[...].astype(o_ref.dtype)


def sb_attention(qkv, *, batch, seq, tq, first_col):
    m = qkv.shape[0]
    d = SB_HEAD_DIM
    tq = min(tq, seq)
    assert tq % SB_BLOCK == 0 and seq >= 2 * SB_BLOCK
    nq = seq // tq
    return pl.pallas_call(
        functools.partial(_sb_attn_kernel, tq=tq),
        grid=(batch, SB_HEADS, nq),
        in_specs=[
            pl.BlockSpec((tq, d), lambda b, h, i: (b * nq + i, first_col + h)),
            pl.BlockSpec((seq, d), lambda b, h, i: (b, first_col + SB_HEADS + h)),
            pl.BlockSpec((seq, d), lambda b, h, i: (b, first_col + 2 * SB_HEADS + h)),
        ],
        out_specs=pl.BlockSpec((tq, d), lambda b, h, i: (b * nq + i, h)),
        out_shape=jax.ShapeDtypeStruct((m, SB_HEADS * d), BF16),
        scratch_shapes=[pltpu.VMEM((tq, 1), F32), pltpu.VMEM((tq, d), F32)],
        compiler_params=_params("parallel", "parallel", "arbitrary"),
        name="sb_attention",
    )(qkv, qkv, qkv)


def _mla_attn_kernel(qt_ref, k_ref, vt_ref, o_ref, s_ref, smax_ref, acc_ref, *, tq, tk):
    i = pl.program_id(2)
    assert tq == 2 * tk

    def scores(blk, slot):
        k_blk = k_ref[pl.ds(pl.multiple_of(blk * tk, tk), tk), :]
        for half in range(2):
            s = jnp.dot(k_blk, qt_ref[half], preferred_element_type=F32)
            s_ref[slot, :, half * tk:(half + 1) * tk] = s
            smax_ref[slot, :, half * tk:(half + 1) * tk] = jnp.max(s, axis=0, keepdims=True)

    def update(blk, slot, m_old, mask):
        s = s_ref[slot]
        if mask is None:
            s_max = smax_ref[slot]
        else:
            s = jnp.where(mask, s, NEG_INF)
            s_max = jnp.max(s, axis=0, keepdims=True)
        m_new = jnp.maximum(m_old, s_max)
        p = jnp.exp2(s - m_new).astype(BF16)
        pv = jnp.dot(vt_ref[blk], p, preferred_element_type=F32)
        acc_ref[...] = jnp.exp2(m_old - m_new) * acc_ref[...] + pv
        return m_new

    acc_ref[...] = jnp.zeros(acc_ref.shape, F32)
    m = jnp.full((1, tq), NEG_INF, F32)
    scores(0, 0)

    def pair(t, m):
        scores(2 * t + 1, 1)
        m = update(2 * t, 0, m, None)
        scores(2 * t + 2, 0)
        return update(2 * t + 1, 1, m, None)

    m = lax.fori_loop(0, i // 2, lambda u, m: pair(2 * u + 1, pair(2 * u, m)), m)
    m = lax.cond(i % 2 == 1, lambda m: pair(i - 1, m), lambda m: m, m)
    k_chunk = lax.broadcasted_iota(jnp.int32, (tk, tq), 0) // CHUNK
    q_chunk = lax.broadcasted_iota(jnp.int32, (tk, tq), 1) // CHUNK
    allowed = k_chunk <= q_chunk
    blk = 2 * i + 1
    s = jnp.dot(k_ref[pl.ds(pl.multiple_of(blk * tk, tk), tk), :], qt_ref[1],
                preferred_element_type=F32)
    m = update(2 * i, 0, m, allowed)
    s = jnp.where(allowed[:, :tk], s, NEG_INF)
    m_old = m[:, tk:]
    m_new = jnp.maximum(m_old, jnp.max(s, axis=0, keepdims=True))
    pv = jnp.dot(vt_ref[blk], jnp.exp2(s - m_new).astype(BF16), preferred_element_type=F32)
    acc_ref[:, tk:] = jnp.exp2(m_old - m_new) * acc_ref[:, tk:] + pv
    out_t = acc_ref[:MLA_V_DIM, :] / acc_ref[MLA_V_DIM:MLA_V_DIM + 1, :]
    o_ref[...] = out_t.T.astype(o_ref.dtype)


def mla_attention(qt, k, vt, *, batch, seq):
    m = k.shape[0]
    _, _, nk, d_ext, tk = vt.shape
    tq = 2 * tk
    assert tk % CHUNK == 0 and nk * tk == seq
    nq = seq // tq
    return pl.pallas_call(
        functools.partial(_mla_attn_kernel, tq=tq, tk=tk),
        grid=(batch, MLA_HEADS, nq),
        in_specs=[
            pl.BlockSpec((None, None, 2, MLA_QK_PAD, tk), lambda b, h, i: (b, h, i, 0, 0)),
            pl.BlockSpec((seq, MLA_QK_PAD), lambda b, h, i: (b, h)),
            pl.BlockSpec((None, None, nk, d_ext, tk), lambda b, h, i: (b, h, 0, 0, 0)),
        ],
        out_specs=pl.BlockSpec((tq, MLA_V_DIM), lambda b, h, i: (b * nq + i, h)),
        out_shape=jax.ShapeDtypeStruct((m, MLA_HEADS * MLA_V_DIM), BF16),
        scratch_shapes=[pltpu.VMEM((2, tk, tq), F32), pltpu.VMEM((2, 1, tq), F32),
                        pltpu.VMEM((d_ext, tq), F32)],
        compiler_params=_params("parallel", "parallel", "arbitrary"),
        name="mla_attention",
    )(qt, k, vt)


def _merge_kernel(x_ref, asb_ref, amla_ref, gl_ref, bg_ref, wosb_ref, womla_ref, wout_ref,
                  gpost_ref, o_ref):
    d = x_ref.shape[1]
    y_sb = jnp.dot(asb_ref[...], wosb_ref[...], preferred_element_type=F32)
    y_mla = jnp.dot(amla_ref[...], womla_ref[...], preferred_element_type=F32)
    gates = jax.nn.sigmoid(gl_ref[...].astype(F32) + bg_ref[...])
    mixed_in = gates[:, :d] * y_sb + gates[:, d:] * y_mla
    mixed = jnp.dot(mixed_in.astype(BF16), wout_ref[...], preferred_element_type=F32)
    o_ref[...] = x_ref[...] + _rms(mixed, gpost_ref[...])


def merge(x, a_sb, a_mla, gate_logits, b_gate, w_o_sb, w_o_mla, w_out, g_post, *, tm):
    m, d = x.shape
    tm = min(tm, m)
    const = lambda i: (0, 0)
    resident = dict(pipeline_mode=pl.Buffered(1))
    return pl.pallas_call(
        _merge_kernel,
        grid=(m // tm,),
        in_specs=[
            pl.BlockSpec((tm, d), lambda i: (i, 0)),
            pl.BlockSpec((tm, a_sb.shape[1]), lambda i: (i, 0)),
            pl.BlockSpec((tm, a_mla.shape[1]), lambda i: (i, 0)),
            pl.BlockSpec((tm, 2 * d), lambda i: (i, 0)),
            pl.BlockSpec(b_gate.shape, const),
            pl.BlockSpec(w_o_sb.shape, const, **resident),
            pl.BlockSpec(w_o_mla.shape, const, **resident),
            pl.BlockSpec(w_out.shape, const, **resident),
            pl.BlockSpec(g_post.shape, const),
        ],
        out_specs=pl.BlockSpec((tm, d), lambda i: (i, 0)),
        out_shape=jax.ShapeDtypeStruct((m, d), F32),
        compiler_params=_params("parallel"),
        name="merge",
    )(x, a_sb, a_mla, gate_logits, b_gate, w_o_sb, w_o_mla, w_out, g_post)


def _ffn_kernel(h_ref, gpre_ref, wg_ref, wu_ref, wd_ref, gpost_ref, o_ref, hn_ref, acc_ref):
    f = pl.program_id(1)

    @pl.when(f == 0)
    def _():
        hn_ref[...] = _rms(h_ref[...], gpre_ref[...]).astype(BF16)
        acc_ref[...] = jnp.zeros_like(acc_ref)

    hn = hn_ref[...]
    g = jnp.dot(hn, wg_ref[...], preferred_element_type=F32)
    up = jnp.dot(hn, wu_ref[...], preferred_element_type=F32)
    act = (g * jax.nn.sigmoid(g)) * up
    acc_ref[...] += jnp.dot(act.astype(BF16), wd_ref[...], preferred_element_type=F32)

    @pl.when(f == pl.num_programs(1) - 1)
    def _():
        o_ref[...] = h_ref[...] + _rms(acc_ref[...], gpost_ref[...])


def ffn(h, g_pre, w_gate_up, w_down, g_post, *, tm, tf):
    m, d = h.shape
    d_ff = w_down.shape[0]
    tm = min(tm, m)
    nf = d_ff // tf
    const = lambda i, f: (0, 0)
    return pl.pallas_call(
        _ffn_kernel,
        grid=(m // tm, nf),
        in_specs=[
            pl.BlockSpec((tm, d), lambda i, f: (i, 0)),
            pl.BlockSpec(g_pre.shape, const),
            pl.BlockSpec((d, tf), lambda i, f: (0, f)),
            pl.BlockSpec((d, tf), lambda i, f: (0, nf + f)),
            pl.BlockSpec((tf, d), lambda i, f: (f, 0)),
            pl.BlockSpec(g_post.shape, const),
        ],
        out_specs=pl.BlockSpec((tm, d), lambda i, f: (i, 0)),
        out_shape=jax.ShapeDtypeStruct((m, d), F32),
        scratch_shapes=[pltpu.VMEM((tm, d), BF16), pltpu.VMEM((tm, d), F32)],
        compiler_params=_params("parallel", "arbitrary"),
        name="ffn",
    )(h, g_pre, w_gate_up, w_gate_up, w_down, g_post)


def _rope_table(positions):
    half = MLA_ROPE_DIM // 2
    inv_freq = ROPE_THETA ** (-jnp.arange(half, dtype=F32) / half)
    ang = positions.astype(F32)[:, None] * inv_freq
    pad = jnp.zeros((positions.shape[0], LANES - MLA_ROPE_DIM), F32)
    return jnp.concatenate([jnp.cos(ang), jnp.sin(ang), pad], axis=1)


def _layer(l, x2, pos, norm_pre_mix, w_in, b_gate, q_norm, w_uq, kv_norm, w_ukv, w_o_sb, w_o_mla,
           w_out, norm_post_mix, norm_pre_ffn, w_gate_up, w_down, norm_post_ffn, *, batch, seq):
    d = x2.shape[1]
    sb_w = SB_HEADS * SB_HEAD_DIM
    o_qlat = 3 * sb_w
    o_kvlat = o_qlat + MLA_Q_RANK
    o_kr = o_kvlat + MLA_KV_RANK
    o_gate = o_kr + MLA_ROPE_DIM
    row = lambda v: v[l][None, :].astype(F32)

    w_in_l = w_in[l]
    w_gate_qkv = jnp.concatenate([w_in_l[:, o_gate:], w_in_l[:, :o_qlat]], axis=1).astype(BF16)
    w_lat = jnp.concatenate(
        [w_in_l[:, o_qlat:o_gate], jnp.zeros((d, LANES - MLA_ROPE_DIM), w_in_l.dtype)], axis=1).astype(BF16)

    qk_dim = MLA_NOPE_DIM + MLA_ROPE_DIM
    wq = w_uq[l].reshape(MLA_Q_RANK, MLA_HEADS, qk_dim)
    wq = jnp.pad(wq, ((0, 0), (0, 0), (0, MLA_QK_PAD - qk_dim)))
    wq = wq.reshape(MLA_Q_RANK, MLA_HEADS * MLA_QK_PAD).astype(BF16)
    wkv = w_ukv[l].reshape(MLA_KV_RANK, MLA_HEADS, MLA_NOPE_DIM + MLA_V_DIM)
    wkv = jnp.concatenate([wkv[:, :, :MLA_NOPE_DIM].reshape(MLA_KV_RANK, -1),
                           wkv[:, :, MLA_NOPE_DIM:].reshape(MLA_KV_RANK, -1)], axis=1).astype(BF16)

    g_pre = row(norm_pre_mix)
    sb_q_scale = -(SB_HEAD_DIM ** -0.5) * LOG2E
    n_gate = w_in_l.shape[1] - o_gate
    col_scale = jnp.concatenate([jnp.ones((1, n_gate), F32), jnp.full((1, sb_w), sb_q_scale, F32),
                                 jnp.ones((1, 2 * sb_w), F32)], axis=1)
    gate_qkv = norm_matmul(x2, g_pre, w_gate_qkv, col_scale, tm=IN_PROJ_TM, tn=IN_PROJ_TN)

    q_m, k_m, vt_m = mla_prep(x2, g_pre, w_lat, row(q_norm), wq, row(kv_norm), wkv, _rope_table(pos),
                              batch=batch, tm=MLA_KEY_BLOCK, q_scale=(qk_dim ** -0.5) * LOG2E)

    a_sb = sb_attention(gate_qkv, batch=batch, seq=seq, tq=SB_QUERY_BLOCK, first_col=n_gate // SB_HEAD_DIM)
    a_mla = mla_attention(q_m, k_m, vt_m, batch=batch, seq=seq)

    h1 = merge(x2, a_sb, a_mla, gate_qkv, row(b_gate), w_o_sb[l].astype(BF16),
               w_o_mla[l].astype(BF16), w_out[l].astype(BF16), row(norm_post_mix), tm=MERGE_TM)
    return ffn(h1, row(norm_pre_ffn), w_gate_up[l].astype(BF16), w_down[l].astype(BF16),
               row(norm_post_ffn), tm=FFN_TM, tf=FFN_TF)


def kernel(x, positions, norm_pre_mix, w_in, b_gate, q_norm, w_uq, kv_norm, w_ukv, w_o_sb, w_o_mla,
           w_out, norm_post_mix, norm_pre_ffn, w_gate_up, w_down, norm_post_ffn):
    batch, seq, d = x.shape
    h = x.reshape(batch * seq, d)
    pos = positions.reshape(batch * seq)
    for l in range(w_in.shape[0]):
        h = _layer(l, h, pos, norm_pre_mix, w_in, b_gate, q_norm, w_uq, kv_norm, w_ukv, w_o_sb,
                   w_o_mla, w_out, norm_post_mix, norm_pre_ffn, w_gate_up, w_down, norm_post_ffn,
                   batch=batch, seq=seq)
    return h.reshape(batch, seq, d)
```

```python
import functools
import math

import jax
import jax.numpy as jnp
from jax import lax
from jax.experimental import pallas as pl
from jax.experimental.pallas import tpu as pltpu

F32 = jnp.float32
BF16 = jnp.bfloat16

CHUNK = 64
SB_HEADS = 8
SB_HEAD_DIM = 128
MLA_HEADS = 8
MLA_NOPE_DIM = 128
MLA_ROPE_DIM = 64
MLA_V_DIM = 128
MLA_Q_RANK = 768
MLA_KV_RANK = 512
ROPE_THETA = 10000.0
NORM_EPS = 1e-6
NEG_INF = -1e30
LOG2E = math.log2(math.e)

LANES = 128
MLA_QK_PAD = 2 * LANES
VMEM_LIMIT = 56 * 1024 * 1024
SB_BLOCK = 256
SB_DEAD_LOG2 = -160.0
BF16_ROWS = 16
MLA_KEY_BLOCK = 512
MLA_Q_PER_STEP = 4
IN_PROJ_TM, IN_PROJ_TN = 1024, 1024
SB_QUERY_BLOCK = 8 * SB_BLOCK
MERGE_TM = 256
FFN_TM, FFN_TF = 512, 512


def _params(*semantics):
    return pltpu.CompilerParams(dimension_semantics=semantics, vmem_limit_bytes=VMEM_LIMIT)


def _rms(x, g):
    ms = jnp.mean(x * x, axis=-1, keepdims=True)
    return x * lax.rsqrt(ms + NORM_EPS) * g


def _norm_matmul_kernel(x_ref, g_ref, w_ref, cs_ref, o_ref, u_ref):
    @pl.when(pl.program_id(1) == 0)
    def _():
        u_ref[...] = _rms(x_ref[...], g_ref[...]).astype(BF16)

    acc = jnp.dot(u_ref[...], w_ref[...], preferred_element_type=F32)
    o_ref[...] = (acc * cs_ref[...]).astype(o_ref.dtype)


def norm_matmul(x, gain, w, col_scale, *, tm, tn):
    m, d = x.shape
    n = w.shape[1]
    tm = min(tm, m)
    return pl.pallas_call(
        _norm_matmul_kernel,
        grid=(m // tm, n // tn),
        in_specs=[
            pl.BlockSpec((tm, d), lambda i, j: (i, 0)),
            pl.BlockSpec((1, d), lambda i, j: (0, 0)),
            pl.BlockSpec((d, tn), lambda i, j: (0, j)),
            pl.BlockSpec((1, tn), lambda i, j: (0, j)),
        ],
        out_specs=pl.BlockSpec((tm, tn), lambda i, j: (i, j)),
        out_shape=jax.ShapeDtypeStruct((m, n), BF16),
        scratch_shapes=[pltpu.VMEM((tm, d), BF16)],
        compiler_params=_params("parallel", "arbitrary"),
        name="norm_matmul",
    )(x, gain, w, col_scale)


def _rope_coeffs(rope_ref):
    half = MLA_ROPE_DIM // 2
    t = rope_ref[...]
    lane = lax.broadcasted_iota(jnp.int32, t.shape, 1)
    first, second = lane < half, jnp.logical_and(lane >= half, lane < 2 * half)
    c = jnp.where(first, t, jnp.where(second, pltpu.roll(t, half, 1), 1.0))
    a = jnp.where(first, -pltpu.roll(t, LANES - half, 1), 0.0)
    b = jnp.where(second, t, 0.0)
    return c, a, b


def _rope128(r, coeffs):
    c, a, b = coeffs
    half = MLA_ROPE_DIM // 2
    return r * c + pltpu.roll(r, LANES - half, 1) * a + pltpu.roll(r, half, 1) * b


def _mla_prep_kernel(x_ref, g_ref, wlat_ref, qn_ref, wuq_ref, kvn_ref, wukv_ref, rope_ref,
                     qt_ref, k_ref, vt_ref, *, q_scale):
    u = _rms(x_ref[...], g_ref[...]).astype(BF16)
    lat = jnp.dot(u, wlat_ref[...], preferred_element_type=F32)
    q_lat = lat[:, :MLA_Q_RANK]
    kv_lat = lat[:, MLA_Q_RANK:MLA_Q_RANK + MLA_KV_RANK]
    k_r = lat[:, MLA_Q_RANK + MLA_KV_RANK:]

    q = jnp.dot(_rms(q_lat, qn_ref[...]).astype(BF16), wuq_ref[...], preferred_element_type=F32)
    kv = jnp.dot(_rms(kv_lat, kvn_ref[...]).astype(BF16), wukv_ref[...], preferred_element_type=F32)
    coeffs = _rope_coeffs(rope_ref)
    k_rope = _rope128(k_r, coeffs).astype(BF16)
    ones = jnp.ones((BF16_ROWS, x_ref.shape[0]), BF16)

    kv_split = MLA_HEADS * MLA_NOPE_DIM
    for h in range(MLA_HEADS):
        lo = h * MLA_QK_PAD
        q_nope = q[:, lo:lo + LANES]
        q_rope = _rope128(q[:, lo + LANES:lo + 2 * LANES], coeffs)
        qt_ref[h, :LANES, :] = (q_nope * q_scale).T.astype(BF16)
        qt_ref[h, LANES:, :] = (q_rope * q_scale).T.astype(BF16)
        k_ref[:, lo:lo + LANES] = kv[:, h * LANES:(h + 1) * LANES].astype(BF16)
        k_ref[:, lo + LANES:lo + 2 * LANES] = k_rope
        v_h = kv[:, kv_split + h * MLA_V_DIM:kv_split + (h + 1) * MLA_V_DIM]
        vt_ref[h, :MLA_V_DIM, :] = v_h.T.astype(BF16)
        vt_ref[h, MLA_V_DIM:, :] = ones


def mla_prep(x, gain, w_lat, q_norm, w_uq, kv_norm, w_ukv, rope_tab, *, batch, tm, q_scale):
    m, d = x.shape
    n_tiles = m // batch // tm
    const = lambda i: (0, 0)
    tile = lambda i: (i // n_tiles, 0, i % n_tiles, 0, 0)
    qk_w = MLA_HEADS * MLA_QK_PAD
    d_ext = MLA_V_DIM + BF16_ROWS
    return pl.pallas_call(
        functools.partial(_mla_prep_kernel, q_scale=q_scale),
        grid=(m // tm,),
        in_specs=[
            pl.BlockSpec((tm, d), lambda i: (i, 0)),
            pl.BlockSpec(gain.shape, const),
            pl.BlockSpec(w_lat.shape, const),
            pl.BlockSpec(q_norm.shape, const),
            pl.BlockSpec(w_uq.shape, const),
            pl.BlockSpec(kv_norm.shape, const),
            pl.BlockSpec(w_ukv.shape, const),
            pl.BlockSpec((tm, LANES), lambda i: (i, 0)),
        ],
        out_specs=[
            pl.BlockSpec((None, MLA_HEADS, None, MLA_QK_PAD, tm), tile),
            pl.BlockSpec((tm, qk_w), lambda i: (i, 0)),
            pl.BlockSpec((None, MLA_HEADS, None, d_ext, tm), tile),
        ],
        out_shape=[
            jax.ShapeDtypeStruct((batch, MLA_HEADS, n_tiles, MLA_QK_PAD, tm), BF16),
            jax.ShapeDtypeStruct((m, qk_w), BF16),
            jax.ShapeDtypeStruct((batch, MLA_HEADS, n_tiles, d_ext, tm), BF16),
        ],
        compiler_params=_params("parallel"),
        name="mla_prep",
    )(x, gain, w_lat, q_norm, w_uq, kv_norm, w_ukv, rope_tab)


def _qkt(q, k):
    return lax.dot_general(q, k, (((1,), (1,)), ((), ())), preferred_element_type=F32)


def _sb_attn_kernel(q_ref, k_ref, v_ref, o_ref, run_ref, acc_ref, *, tq):
    tb = SB_BLOCK
    i = pl.program_id(2)
    subs = range(tq // tb)
    row = lax.broadcasted_iota(jnp.int32, (tb, tb), 0)
    col = lax.broadcasted_iota(jnp.int32, (tb, tb), 1)
    before = row > col
    later = before.astype(BF16)

    def log2_keep(nz):
        return jnp.minimum(nz, 0.0) - jnp.log(1.0 + jnp.exp2(-jnp.abs(nz))) * LOG2E

    def suffix(keep):
        return jnp.dot(keep.astype(BF16), later, preferred_element_type=F32)

    q_first = [i * tq + g * tb for g in subs]
    starts = [pl.multiple_of(jnp.maximum(q_first[g] - tb, 0), tb) for g in subs]
    nz = [_qkt(q_ref[g * tb:(g + 1) * tb, :], k_ref[pl.ds(starts[g], 2 * tb), :]) for g in subs]
    nz_lo = [x[:, :tb] for x in nz]
    nz_hi = [x[:, tb:] for x in nz]
    delta = q_first[0] - starts[0]
    valid_lo = [(col - row) < delta] + [None for _ in subs[1:]]
    valid_hi = [(col - row) < delta - tb] + [before for _ in subs[1:]]
    keep_lo = [log2_keep(x) for x in nz_lo]
    keep_lo[0] = jnp.where(valid_lo[0], keep_lo[0], 0.0)
    keep_hi = [jnp.where(valid_hi[g], log2_keep(nz_hi[g]), 0.0) for g in subs]
    suf_hi = [suffix(x) for x in keep_hi]
    suf_lo = [suffix(x) for x in keep_lo]
    run_lo = [suf_hi[g][:, 0:1] + keep_hi[g][:, 0:1] for g in subs]
    w_hi = [jnp.where(valid_hi[g], jnp.exp2((keep_hi[g] - nz_hi[g]) + suf_hi[g]), 0.0) for g in subs]
    w_lo = [jnp.exp2((keep_lo[g] - nz_lo[g]) + suf_lo[g] + run_lo[g]) for g in subs]
    w_lo[0] = jnp.where(valid_lo[0], w_lo[0], 0.0)
    for g in subs:
        w = jnp.concatenate([w_lo[g], w_hi[g]], axis=1).astype(BF16)
        acc_ref[g * tb:(g + 1) * tb, :] = jnp.dot(w, v_ref[pl.ds(starts[g], 2 * tb), :],
                                                  preferred_element_type=F32)
        run_ref[g * tb:(g + 1) * tb, :] = run_lo[g] + suf_lo[g][:, 0:1] + keep_lo[g][:, 0:1]

    def walk_back(g):
        rows = pl.ds(g * tb, tb)

        def alive(c):
            return jnp.logical_and(c[0] >= 0, c[1] > SB_DEAD_LOG2)

        def body(c):
            start = pl.multiple_of(c[0] * tb, tb)
            nz = _qkt(q_ref[rows, :], k_ref[pl.ds(start, tb), :])
            keep = log2_keep(nz)
            suf = suffix(keep)
            run = run_ref[rows, :]
            w = jnp.exp2((keep - nz) + suf + run)
            acc_ref[rows, :] += jnp.dot(w.astype(BF16), v_ref[pl.ds(start, tb), :],
                                        preferred_element_type=F32)
            run = run + suf[:, 0:1] + keep[:, 0:1]
            run_ref[rows, :] = run
            return c[0] - 1, jnp.max(run)

        lax.while_loop(alive, body, (i * len(subs) + g - 2, jnp.max(run_ref[rows, :])))

    @pl.when(jnp.max(run_ref[...]) > SB_DEAD_LOG2)
    def _():
        for g in subs:
            walk_back(g)

    o_ref[...] = acc_ref[...].astype(o_ref.dtype)


def sb_attention(qkv, *, batch, seq, tq, first_col):
    m = qkv.shape[0]
    d = SB_HEAD_DIM
    tq = min(tq, seq)
    assert tq % SB_BLOCK == 0 and seq >= 2 * SB_BLOCK
    nq = seq // tq
    return pl.pallas_call(
        functools.partial(_sb_attn_kernel, tq=tq),
        grid=(batch, SB_HEADS, nq),
        in_specs=[
            pl.BlockSpec((tq, d), lambda b, h, i: (b * nq + i, first_col + h)),
            pl.BlockSpec((seq, d), lambda b, h, i: (b, first_col + SB_HEADS + h)),
            pl.BlockSpec((seq, d), lambda b, h, i: (b, first_col + 2 * SB_HEADS + h)),
        ],
        out_specs=pl.BlockSpec((tq, d), lambda b, h, i: (b * nq + i, h)),
        out_shape=jax.ShapeDtypeStruct((m, SB_HEADS * d), BF16),
        scratch_shapes=[pltpu.VMEM((tq, 1), F32), pltpu.VMEM((tq, d), F32)],
        compiler_params=_params("parallel", "parallel", "arbitrary"),
        name="sb_attention",
    )(qkv, qkv, qkv)


def _mla_attn_kernel(qt_ref, k_ref, vt_ref, o_ref, s_ref, smax_ref, acc_ref, *, tq, tk):
    for j in range(MLA_Q_PER_STEP):
        _mla_query_block(pl.program_id(2) * MLA_Q_PER_STEP + j, qt_ref.at[2 * j:2 * j + 2], k_ref, vt_ref,
                         o_ref.at[j * tq:(j + 1) * tq], s_ref.at[j], smax_ref.at[j], acc_ref.at[j],
                         tq=tq, tk=tk)


def _mla_query_block(i, qt_ref, k_ref, vt_ref, o_ref, s_ref, smax_ref, acc_ref, *, tq, tk):
    assert tq == 2 * tk

    def scores(blk, slot):
        k_blk = k_ref[pl.ds(pl.multiple_of(blk * tk, tk), tk), :]
        for half in range(2):
            s = jnp.dot(k_blk, qt_ref[half], preferred_element_type=F32)
            s_ref[slot, :, half * tk:(half + 1) * tk] = s
            smax_ref[slot, :, half * tk:(half + 1) * tk] = jnp.max(s, axis=0, keepdims=True)

    def update(blk, slot, m_old, mask):
        s = s_ref[slot]
        if mask is None:
            s_max = smax_ref[slot]
        else:
            s = jnp.where(mask, s, NEG_INF)
            s_max = jnp.max(s, axis=0, keepdims=True)
        m_new = jnp.maximum(m_old, s_max)
        p = jnp.exp2(s - m_new).astype(BF16)
        pv = jnp.dot(vt_ref[blk], p, preferred_element_type=F32)
        acc_ref[...] = jnp.exp2(m_old - m_new) * acc_ref[...] + pv
        return m_new

    acc_ref[...] = jnp.zeros(acc_ref.shape, F32)
    m = jnp.full((1, tq), NEG_INF, F32)
    scores(0, 0)

    def pair(t, m):
        scores(2 * t + 1, 1)
        m = update(2 * t, 0, m, None)
        scores(2 * t + 2, 0)
        return update(2 * t + 1, 1, m, None)

    m = lax.fori_loop(0, i // 2, lambda u, m: pair(2 * u + 1, pair(2 * u, m)), m)
    m = lax.cond(i % 2 == 1, lambda m: pair(i - 1, m), lambda m: m, m)
    k_chunk = lax.broadcasted_iota(jnp.int32, (tk, tq), 0) // CHUNK
    q_chunk = lax.broadcasted_iota(jnp.int32, (tk, tq), 1) // CHUNK
    allowed = k_chunk <= q_chunk
    blk = 2 * i + 1
    s = jnp.dot(k_ref[pl.ds(pl.multiple_of(blk * tk, tk), tk), :], qt_ref[1],
                preferred_element_type=F32)
    m = update(2 * i, 0, m, allowed)
    s = jnp.where(allowed[:, :tk], s, NEG_INF)
    m_old = m[:, tk:]
    m_new = jnp.maximum(m_old, jnp.max(s, axis=0, keepdims=True))
    pv = jnp.dot(vt_ref[blk], jnp.exp2(s - m_new).astype(BF16), preferred_element_type=F32)
    acc_ref[:, tk:] = jnp.exp2(m_old - m_new) * acc_ref[:, tk:] + pv
    out_t = acc_ref[:MLA_V_DIM, :] / acc_ref[MLA_V_DIM:MLA_V_DIM + 1, :]
    o_ref[...] = out_t.T.astype(o_ref.dtype)


def mla_attention(qt, k, vt, *, batch, seq):
    m = k.shape[0]
    _, _, nk, d_ext, tk = vt.shape
    tq = 2 * tk
    assert tk % CHUNK == 0 and nk * tk == seq
    per = MLA_Q_PER_STEP
    nq = seq // (per * tq)
    return pl.pallas_call(
        functools.partial(_mla_attn_kernel, tq=tq, tk=tk),
        grid=(batch, MLA_HEADS, nq),
        in_specs=[
            pl.BlockSpec((None, None, 2 * per, MLA_QK_PAD, tk), lambda b, h, i: (b, h, i, 0, 0)),
            pl.BlockSpec((seq, MLA_QK_PAD), lambda b, h, i: (b, h)),
            pl.BlockSpec((None, None, nk, d_ext, tk), lambda b, h, i: (b, h, 0, 0, 0)),
        ],
        out_specs=pl.BlockSpec((per * tq, MLA_V_DIM), lambda b, h, i: (b * nq + i, h)),
        out_shape=jax.ShapeDtypeStruct((m, MLA_HEADS * MLA_V_DIM), BF16),
        scratch_shapes=[pltpu.VMEM((per, 2, tk, tq), F32), pltpu.VMEM((per, 2, 1, tq), F32),
                        pltpu.VMEM((per, d_ext, tq), F32)],
        compiler_params=_params("parallel", "parallel", "arbitrary"),
        name="mla_attention",
    )(qt, k, vt)


def _merge_kernel(x_ref, asb_ref, amla_ref, gl_ref, bg_ref, wosb_ref, womla_ref, wout_ref,
                  gpost_ref, o_ref):
    d = x_ref.shape[1]
    y_sb = jnp.dot(asb_ref[...], wosb_ref[...], preferred_element_type=F32)
    y_mla = jnp.dot(amla_ref[...], womla_ref[...], preferred_element_type=F32)
    gates = jax.nn.sigmoid(gl_ref[...].astype(F32) + bg_ref[...])
    mixed_in = gates[:, :d] * y_sb + gates[:, d:] * y_mla
    mixed = jnp.dot(mixed_in.astype(BF16), wout_ref[...], preferred_element_type=F32)
    o_ref[...] = x_ref[...] + _rms(mixed, gpost_ref[...])


def merge(x, a_sb, a_mla, gate_logits, b_gate, w_o_sb, w_o_mla, w_out, g_post, *, tm):
    m, d = x.shape
    tm = min(tm, m)
    const = lambda i: (0, 0)
    resident = dict(pipeline_mode=pl.Buffered(1))
    return pl.pallas_call(
        _merge_kernel,
        grid=(m // tm,),
        in_specs=[
            pl.BlockSpec((tm, d), lambda i: (i, 0)),
            pl.BlockSpec((tm, a_sb.shape[1]), lambda i: (i, 0)),
            pl.BlockSpec((tm, a_mla.shape[1]), lambda i: (i, 0)),
            pl.BlockSpec((tm, 2 * d), lambda i: (i, 0)),
            pl.BlockSpec(b_gate.shape, const),
            pl.BlockSpec(w_o_sb.shape, const, **resident),
            pl.BlockSpec(w_o_mla.shape, const, **resident),
            pl.BlockSpec(w_out.shape, const, **resident),
            pl.BlockSpec(g_post.shape, const),
        ],
        out_specs=pl.BlockSpec((tm, d), lambda i: (i, 0)),
        out_shape=jax.ShapeDtypeStruct((m, d), F32),
        compiler_params=_params("parallel"),
        name="merge",
    )(x, a_sb, a_mla, gate_logits, b_gate, w_o_sb, w_o_mla, w_out, g_post)


def _ffn_kernel(h_ref, gpre_ref, wg_ref, wu_ref, wd_ref, gpost_ref, o_ref, hn_ref, acc_ref):
    f = pl.program_id(1)

    @pl.when(f == 0)
    def _():
        hn_ref[...] = _rms(h_ref[...], gpre_ref[...]).astype(BF16)
        acc_ref[...] = jnp.zeros_like(acc_ref)

    hn = hn_ref[...]
    g = jnp.dot(hn, wg_ref[...], preferred_element_type=F32)
    up = jnp.dot(hn, wu_ref[...], preferred_element_type=F32)
    act = (g * jax.nn.sigmoid(g)) * up
    acc_ref[...] += jnp.dot(act.astype(BF16), wd_ref[...], preferred_element_type=F32)

    @pl.when(f == pl.num_programs(1) - 1)
    def _():
        o_ref[...] = h_ref[...] + _rms(acc_ref[...], gpost_ref[...])


def ffn(h, g_pre, w_gate_up, w_down, g_post, *, tm, tf):
    m, d = h.shape
    d_ff = w_down.shape[0]
    tm = min(tm, m)
    nf = d_ff // tf
    const = lambda i, f: (0, 0)
    return pl.pallas_call(
        _ffn_kernel,
        grid=(m // tm, nf),
        in_specs=[
            pl.BlockSpec((tm, d), lambda i, f: (i, 0)),
            pl.BlockSpec(g_pre.shape, const),
            pl.BlockSpec((d, tf), lambda i, f: (0, f)),
            pl.BlockSpec((d, tf), lambda i, f: (0, nf + f)),
            pl.BlockSpec((tf, d), lambda i, f: (f, 0)),
            pl.BlockSpec(g_post.shape, const),
        ],
        out_specs=pl.BlockSpec((tm, d), lambda i, f: (i, 0)),
        out_shape=jax.ShapeDtypeStruct((m, d), F32),
        scratch_shapes=[pltpu.VMEM((tm, d), BF16), pltpu.VMEM((tm, d), F32)],
        compiler_params=_params("parallel", "arbitrary"),
        name="ffn",
    )(h, g_pre, w_gate_up, w_gate_up, w_down, g_post)


def _rope_table(positions):
    half = MLA_ROPE_DIM // 2
    inv_freq = ROPE_THETA ** (-jnp.arange(half, dtype=F32) / half)
    ang = positions.astype(F32)[:, None] * inv_freq
    pad = jnp.zeros((positions.shape[0], LANES - MLA_ROPE_DIM), F32)
    return jnp.concatenate([jnp.cos(ang), jnp.sin(ang), pad], axis=1)


def _layer(l, x2, pos, norm_pre_mix, w_in, b_gate, q_norm, w_uq, kv_norm, w_ukv, w_o_sb, w_o_mla,
           w_out, norm_post_mix, norm_pre_ffn, w_gate_up, w_down, norm_post_ffn, *, batch, seq):
    d = x2.shape[1]
    sb_w = SB_HEADS * SB_HEAD_DIM
    o_qlat = 3 * sb_w
    o_kvlat = o_qlat + MLA_Q_RANK
    o_kr = o_kvlat + MLA_KV_RANK
    o_gate = o_kr + MLA_ROPE_DIM
    row = lambda v: v[l][None, :].astype(F32)

    w_in_l = w_in[l]
    w_gate_qkv = jnp.concatenate([w_in_l[:, o_gate:], w_in_l[:, :o_qlat]], axis=1).astype(BF16)
    w_lat = jnp.concatenate(
        [w_in_l[:, o_qlat:o_gate], jnp.zeros((d, LANES - MLA_ROPE_DIM), w_in_l.dtype)], axis=1).astype(BF16)

    qk_dim = MLA_NOPE_DIM + MLA_ROPE_DIM
    wq = w_uq[l].reshape(MLA_Q_RANK, MLA_HEADS, qk_dim)
    wq = jnp.pad(wq, ((0, 0), (0, 0), (0, MLA_QK_PAD - qk_dim)))
    wq = wq.reshape(MLA_Q_RANK, MLA_HEADS * MLA_QK_PAD).astype(BF16)
    wkv = w_ukv[l].reshape(MLA_KV_RANK, MLA_HEADS, MLA_NOPE_DIM + MLA_V_DIM)
    wkv = jnp.concatenate([wkv[:, :, :MLA_NOPE_DIM].reshape(MLA_KV_RANK, -1),
                           wkv[:, :, MLA_NOPE_DIM:].reshape(MLA_KV_RANK, -1)], axis=1).astype(BF16)

    g_pre = row(norm_pre_mix)
    sb_q_scale = -(SB_HEAD_DIM ** -0.5) * LOG2E
    n_gate = w_in_l.shape[1] - o_gate
    col_scale = jnp.concatenate([jnp.ones((1, n_gate), F32), jnp.full((1, sb_w), sb_q_scale, F32),
                                 jnp.ones((1, 2 * sb_w), F32)], axis=1)
    gate_qkv = norm_matmul(x2, g_pre, w_gate_qkv, col_scale, tm=IN_PROJ_TM, tn=IN_PROJ_TN)

    q_m, k_m, vt_m = mla_prep(x2, g_pre, w_lat, row(q_norm), wq, row(kv_norm), wkv, _rope_table(pos),
                              batch=batch, tm=MLA_KEY_BLOCK, q_scale=(qk_dim ** -0.5) * LOG2E)

    a_sb = sb_attention(gate_qkv, batch=batch, seq=seq, tq=SB_QUERY_BLOCK, first_col=n_gate // SB_HEAD_DIM)
    a_mla = mla_attention(q_m, k_m, vt_m, batch=batch, seq=seq)

    h1 = merge(x2, a_sb, a_mla, gate_qkv, row(b_gate), w_o_sb[l].astype(BF16),
               w_o_mla[l].astype(BF16), w_out[l].astype(BF16), row(norm_post_mix), tm=MERGE_TM)
    return ffn(h1, row(norm_pre_ffn), w_gate_up[l].astype(BF16), w_down[l].astype(BF16),
               row(norm_post_ffn), tm=FFN_TM, tf=FFN_TF)


def kernel(x, positions, norm_pre_mix, w_in, b_gate, q_norm, w_uq, kv_norm, w_ukv, w_o_sb, w_o_mla,
           w_out, norm_post_mix, norm_pre_ffn, w_gate_up, w_down, norm_post_ffn):
    batch, seq, d = x.shape
    h = x.reshape(batch * seq, d)
    pos = positions.reshape(batch * seq)
    for l in range(w_in.shape[0]):
        h = _layer(l, h, pos, norm_pre_mix, w_in, b_gate, q_norm, w_uq, kv_norm, w_ukv, w_o_sb,
                   w_o_mla, w_out, norm_post_mix, norm_pre_ffn, w_gate_up, w_down, norm_post_ffn,
                   batch=batch, seq=seq)
    return h.reshape(batch, seq, d)
```

```python
import functools
import math

import jax
import jax.numpy as jnp
from jax import lax
from jax.experimental import pallas as pl
from jax.experimental.pallas import tpu as pltpu

F32 = jnp.float32
BF16 = jnp.bfloat16

CHUNK = 64
SB_HEADS = 8
SB_HEAD_DIM = 128
MLA_HEADS = 8
MLA_NOPE_DIM = 128
MLA_ROPE_DIM = 64
MLA_V_DIM = 128
MLA_Q_RANK = 768
MLA_KV_RANK = 512
ROPE_THETA = 10000.0
NORM_EPS = 1e-6
NEG_INF = -1e30
LOG2E = math.log2(math.e)

LANES = 128
MLA_QK_PAD = 2 * LANES
VMEM_LIMIT = 56 * 1024 * 1024
SB_BLOCK = 256
SB_DEAD_LOG2 = -160.0
BF16_ROWS = 16
MLA_KEY_BLOCK = 512
MLA_Q_PER_STEP = 4
MLA_PREP_TILES_PER_STEP = 2
IN_PROJ_TM, IN_PROJ_TN = 1024, 1024
SB_QUERY_BLOCK = 8 * SB_BLOCK
MERGE_TM, MERGE_SUB_ROWS = 512, 256
FFN_TM, FFN_TF = 512, 512


def _params(*semantics):
    return pltpu.CompilerParams(dimension_semantics=semantics, vmem_limit_bytes=VMEM_LIMIT)


def _rms(x, g):
    ms = jnp.mean(x * x, axis=-1, keepdims=True)
    return x * lax.rsqrt(ms + NORM_EPS) * g


def _norm_matmul_kernel(x_ref, g_ref, w_ref, cs_ref, o_ref, u_ref):
    @pl.when(pl.program_id(1) == 0)
    def _():
        u_ref[...] = _rms(x_ref[...], g_ref[...]).astype(BF16)

    acc = jnp.dot(u_ref[...], w_ref[...], preferred_element_type=F32)
    o_ref[...] = (acc * cs_ref[...]).astype(o_ref.dtype)


def norm_matmul(x, gain, w, col_scale, *, tm, tn):
    m, d = x.shape
    n = w.shape[1]
    tm = min(tm, m)
    return pl.pallas_call(
        _norm_matmul_kernel,
        grid=(m // tm, n // tn),
        in_specs=[
            pl.BlockSpec((tm, d), lambda i, j: (i, 0)),
            pl.BlockSpec((1, d), lambda i, j: (0, 0)),
            pl.BlockSpec((d, tn), lambda i, j: (0, j)),
            pl.BlockSpec((1, tn), lambda i, j: (0, j)),
        ],
        out_specs=pl.BlockSpec((tm, tn), lambda i, j: (i, j)),
        out_shape=jax.ShapeDtypeStruct((m, n), BF16),
        scratch_shapes=[pltpu.VMEM((tm, d), BF16)],
        compiler_params=_params("parallel", "arbitrary"),
        name="norm_matmul",
    )(x, gain, w, col_scale)


def _rope_coeffs(rope_ref):
    half = MLA_ROPE_DIM // 2
    t = rope_ref[...]
    lane = lax.broadcasted_iota(jnp.int32, t.shape, 1)
    first, second = lane < half, jnp.logical_and(lane >= half, lane < 2 * half)
    c = jnp.where(first, t, jnp.where(second, pltpu.roll(t, half, 1), 1.0))
    a = jnp.where(first, -pltpu.roll(t, LANES - half, 1), 0.0)
    b = jnp.where(second, t, 0.0)
    return c, a, b


def _rope128(r, coeffs):
    c, a, b = coeffs
    half = MLA_ROPE_DIM // 2
    return r * c + pltpu.roll(r, LANES - half, 1) * a + pltpu.roll(r, half, 1) * b


def _mla_prep_kernel(x_ref, g_ref, wlat_ref, qn_ref, wuq_ref, kvn_ref, wukv_ref, rope_ref,
                     qt_ref, k_ref, vt_ref, *, q_scale, tm):
    for j in range(x_ref.shape[0] // tm):
        r = slice(j * tm, (j + 1) * tm)
        _mla_prep_tile(x_ref.at[r], g_ref, wlat_ref, qn_ref, wuq_ref, kvn_ref, wukv_ref, rope_ref.at[r],
                       qt_ref.at[:, j], k_ref.at[r], vt_ref.at[:, j], q_scale=q_scale)


def _mla_prep_tile(x_ref, g_ref, wlat_ref, qn_ref, wuq_ref, kvn_ref, wukv_ref, rope_ref,
                   qt_ref, k_ref, vt_ref, *, q_scale):
    u = _rms(x_ref[...], g_ref[...]).astype(BF16)
    lat = jnp.dot(u, wlat_ref[...], preferred_element_type=F32)
    q_lat = lat[:, :MLA_Q_RANK]
    kv_lat = lat[:, MLA_Q_RANK:MLA_Q_RANK + MLA_KV_RANK]
    k_r = lat[:, MLA_Q_RANK + MLA_KV_RANK:]

    q = jnp.dot(_rms(q_lat, qn_ref[...]).astype(BF16), wuq_ref[...], preferred_element_type=F32)
    kv = jnp.dot(_rms(kv_lat, kvn_ref[...]).astype(BF16), wukv_ref[...], preferred_element_type=F32)
    coeffs = _rope_coeffs(rope_ref)
    k_rope = _rope128(k_r, coeffs).astype(BF16)
    ones = jnp.ones((BF16_ROWS, x_ref.shape[0]), BF16)

    kv_split = MLA_HEADS * MLA_NOPE_DIM
    for h in range(MLA_HEADS):
        lo = h * MLA_QK_PAD
        q_nope = q[:, lo:lo + LANES]
        q_rope = _rope128(q[:, lo + LANES:lo + 2 * LANES], coeffs)
        qt_ref[h, :LANES, :] = (q_nope * q_scale).T.astype(BF16)
        qt_ref[h, LANES:, :] = (q_rope * q_scale).T.astype(BF16)
        k_ref[:, lo:lo + LANES] = kv[:, h * LANES:(h + 1) * LANES].astype(BF16)
        k_ref[:, lo + LANES:lo + 2 * LANES] = k_rope
        v_h = kv[:, kv_split + h * MLA_V_DIM:kv_split + (h + 1) * MLA_V_DIM]
        vt_ref[h, :MLA_V_DIM, :] = v_h.T.astype(BF16)
        vt_ref[h, MLA_V_DIM:, :] = ones


def mla_prep(x, gain, w_lat, q_norm, w_uq, kv_norm, w_ukv, rope_tab, *, batch, tm, q_scale):
    m, d = x.shape
    n_tiles = m // batch // tm
    per = MLA_PREP_TILES_PER_STEP
    steps = n_tiles // per
    const = lambda i: (0, 0)
    once = dict(pipeline_mode=pl.Buffered(1))
    tile = lambda i: (i // steps, 0, i % steps, 0, 0)
    qk_w = MLA_HEADS * MLA_QK_PAD
    d_ext = MLA_V_DIM + BF16_ROWS
    return pl.pallas_call(
        functools.partial(_mla_prep_kernel, q_scale=q_scale, tm=tm),
        grid=(m // (per * tm),),
        in_specs=[
            pl.BlockSpec((per * tm, d), lambda i: (i, 0)),
            pl.BlockSpec(gain.shape, const),
            pl.BlockSpec(w_lat.shape, const, **once),
            pl.BlockSpec(q_norm.shape, const),
            pl.BlockSpec(w_uq.shape, const, **once),
            pl.BlockSpec(kv_norm.shape, const),
            pl.BlockSpec(w_ukv.shape, const, **once),
            pl.BlockSpec((per * tm, LANES), lambda i: (i, 0)),
        ],
        out_specs=[
            pl.BlockSpec((None, MLA_HEADS, per, MLA_QK_PAD, tm), tile),
            pl.BlockSpec((per * tm, qk_w), lambda i: (i, 0)),
            pl.BlockSpec((None, MLA_HEADS, per, d_ext, tm), tile),
        ],
        out_shape=[
            jax.ShapeDtypeStruct((batch, MLA_HEADS, n_tiles, MLA_QK_PAD, tm), BF16),
            jax.ShapeDtypeStruct((m, qk_w), BF16),
            jax.ShapeDtypeStruct((batch, MLA_HEADS, n_tiles, d_ext, tm), BF16),
        ],
        compiler_params=_params("parallel"),
        name="mla_prep",
    )(x, gain, w_lat, q_norm, w_uq, kv_norm, w_ukv, rope_tab)


def _qkt(q, k):
    return lax.dot_general(q, k, (((1,), (1,)), ((), ())), preferred_element_type=F32)


def _sb_attn_kernel(q_ref, k_ref, v_ref, o_ref, run_ref, acc_ref, *, tq):
    tb = SB_BLOCK
    i = pl.program_id(2)
    subs = range(tq // tb)
    row = lax.broadcasted_iota(jnp.int32, (tb, tb), 0)
    col = lax.broadcasted_iota(jnp.int32, (tb, tb), 1)
    before = row > col
    later = before.astype(BF16)

    def log2_keep(nz):
        return jnp.minimum(nz, 0.0) - jnp.log(1.0 + jnp.exp2(-jnp.abs(nz))) * LOG2E

    def suffix(keep):
        return jnp.dot(keep.astype(BF16), later, preferred_element_type=F32)

    q_first = [i * tq + g * tb for g in subs]
    starts = [pl.multiple_of(jnp.maximum(q_first[g] - tb, 0), tb) for g in subs]
    nz = [_qkt(q_ref[g * tb:(g + 1) * tb, :], k_ref[pl.ds(starts[g], 2 * tb), :]) for g in subs]
    nz_lo = [x[:, :tb] for x in nz]
    nz_hi = [x[:, tb:] for x in nz]
    delta = q_first[0] - starts[0]
    valid_lo = [(col - row) < delta] + [None for _ in subs[1:]]
    valid_hi = [(col - row) < delta - tb] + [before for _ in subs[1:]]
    keep_lo = [log2_keep(x) for x in nz_lo]
    keep_lo[0] = jnp.where(valid_lo[0], keep_lo[0], 0.0)
    keep_hi = [jnp.where(valid_hi[g], log2_keep(nz_hi[g]), 0.0) for g in subs]
    suf_hi = [suffix(x) for x in keep_hi]
    suf_lo = [suffix(x) for x in keep_lo]
    run_lo = [suf_hi[g][:, 0:1] + keep_hi[g][:, 0:1] for g in subs]
    w_hi = [jnp.where(valid_hi[g], jnp.exp2((keep_hi[g] - nz_hi[g]) + suf_hi[g]), 0.0) for g in subs]
    w_lo = [jnp.exp2((keep_lo[g] - nz_lo[g]) + suf_lo[g] + run_lo[g]) for g in subs]
    w_lo[0] = jnp.where(valid_lo[0], w_lo[0], 0.0)
    for g in subs:
        w = jnp.concatenate([w_lo[g], w_hi[g]], axis=1).astype(BF16)
        acc_ref[g * tb:(g + 1) * tb, :] = jnp.dot(w, v_ref[pl.ds(starts[g], 2 * tb), :],
                                                  preferred_element_type=F32)
        run_ref[g * tb:(g + 1) * tb, :] = run_lo[g] + suf_lo[g][:, 0:1] + keep_lo[g][:, 0:1]

    def walk_back(g):
        rows = pl.ds(g * tb, tb)

        def alive(c):
            return jnp.logical_and(c[0] >= 0, c[1] > SB_DEAD_LOG2)

        def body(c):
            start = pl.multiple_of(c[0] * tb, tb)
            nz = _qkt(q_ref[rows, :], k_ref[pl.ds(start, tb), :])
            keep = log2_keep(nz)
            suf = suffix(keep)
            run = run_ref[rows, :]
            w = jnp.exp2((keep - nz) + suf + run)
            acc_ref[rows, :] += jnp.dot(w.astype(BF16), v_ref[pl.ds(start, tb), :],
                                        preferred_element_type=F32)
            run = run + suf[:, 0:1] + keep[:, 0:1]
            run_ref[rows, :] = run
            return c[0] - 1, jnp.max(run)

        lax.while_loop(alive, body, (i * len(subs) + g - 2, jnp.max(run_ref[rows, :])))

    @pl.when(jnp.max(run_ref[...]) > SB_DEAD_LOG2)
    def _():
        for g in subs:
            walk_back(g)

    o_ref[...] = acc_ref[...].astype(o_ref.dtype)


def sb_attention(qkv, *, batch, seq, tq, first_col):
    m = qkv.shape[0]
    d = SB_HEAD_DIM
    tq = min(tq, seq)
    assert tq % SB_BLOCK == 0 and seq >= 2 * SB_BLOCK
    nq = seq // tq
    return pl.pallas_call(
        functools.partial(_sb_attn_kernel, tq=tq),
        grid=(batch, SB_HEADS, nq),
        in_specs=[
            pl.BlockSpec((tq, d), lambda b, h, i: (b * nq + i, first_col + h)),
            pl.BlockSpec((seq, d), lambda b, h, i: (b, first_col + SB_HEADS + h)),
            pl.BlockSpec((seq, d), lambda b, h, i: (b, first_col + 2 * SB_HEADS + h)),
        ],
        out_specs=pl.BlockSpec((tq, d), lambda b, h, i: (b * nq + i, h)),
        out_shape=jax.ShapeDtypeStruct((m, SB_HEADS * d), BF16),
        scratch_shapes=[pltpu.VMEM((tq, 1), F32), pltpu.VMEM((tq, d), F32)],
        compiler_params=_params("parallel", "parallel", "arbitrary"),
        name="sb_attention",
    )(qkv, qkv, qkv)


def _mla_attn_kernel(qt_ref, k_ref, vt_ref, o_ref, s_ref, smax_ref, acc_ref, *, tq, tk):
    for j in range(MLA_Q_PER_STEP):
        _mla_query_block(pl.program_id(2) * MLA_Q_PER_STEP + j, qt_ref.at[2 * j:2 * j + 2], k_ref, vt_ref,
                         o_ref.at[j * tq:(j + 1) * tq], s_ref.at[j], smax_ref.at[j], acc_ref.at[j],
                         tq=tq, tk=tk)


def _mla_query_block(i, qt_ref, k_ref, vt_ref, o_ref, s_ref, smax_ref, acc_ref, *, tq, tk):
    assert tq == 2 * tk

    def scores(blk, slot):
        k_blk = k_ref[pl.ds(pl.multiple_of(blk * tk, tk), tk), :]
        for half in range(2):
            s = jnp.dot(k_blk, qt_ref[half], preferred_element_type=F32)
            s_ref[slot, :, half * tk:(half + 1) * tk] = s
            smax_ref[slot, :, half * tk:(half + 1) * tk] = jnp.max(s, axis=0, keepdims=True)

    def update(blk, slot, m_old, mask):
        s = s_ref[slot]
        if mask is None:
            s_max = smax_ref[slot]
        else:
            s = jnp.where(mask, s, NEG_INF)
            s_max = jnp.max(s, axis=0, keepdims=True)
        m_new = jnp.maximum(m_old, s_max)
        p = jnp.exp2(s - m_new).astype(BF16)
        pv = jnp.dot(vt_ref[blk], p, preferred_element_type=F32)
        acc_ref[...] = jnp.exp2(m_old - m_new) * acc_ref[...] + pv
        return m_new

    acc_ref[...] = jnp.zeros(acc_ref.shape, F32)
    m = jnp.full((1, tq), NEG_INF, F32)
    scores(0, 0)

    def pair(t, m):
        scores(2 * t + 1, 1)
        m = update(2 * t, 0, m, None)
        scores(2 * t + 2, 0)
        return update(2 * t + 1, 1, m, None)

    m = lax.fori_loop(0, i // 2, lambda u, m: pair(2 * u + 1, pair(2 * u, m)), m)
    m = lax.cond(i % 2 == 1, lambda m: pair(i - 1, m), lambda m: m, m)
    k_chunk = lax.broadcasted_iota(jnp.int32, (tk, tq), 0) // CHUNK
    q_chunk = lax.broadcasted_iota(jnp.int32, (tk, tq), 1) // CHUNK
    allowed = k_chunk <= q_chunk
    blk = 2 * i + 1
    s = jnp.dot(k_ref[pl.ds(pl.multiple_of(blk * tk, tk), tk), :], qt_ref[1],
                preferred_element_type=F32)
    m = update(2 * i, 0, m, allowed)
    s = jnp.where(allowed[:, :tk], s, NEG_INF)
    m_old = m[:, tk:]
    m_new = jnp.maximum(m_old, jnp.max(s, axis=0, keepdims=True))
    pv = jnp.dot(vt_ref[blk], jnp.exp2(s - m_new).astype(BF16), preferred_element_type=F32)
    acc_ref[:, tk:] = jnp.exp2(m_old - m_new) * acc_ref[:, tk:] + pv
    out_t = acc_ref[:MLA_V_DIM, :] / acc_ref[MLA_V_DIM:MLA_V_DIM + 1, :]
    o_ref[...] = out_t.T.astype(o_ref.dtype)


def mla_attention(qt, k, vt, *, batch, seq):
    m = k.shape[0]
    _, _, nk, d_ext, tk = vt.shape
    tq = 2 * tk
    assert tk % CHUNK == 0 and nk * tk == seq
    per = MLA_Q_PER_STEP
    nq = seq // (per * tq)
    return pl.pallas_call(
        functools.partial(_mla_attn_kernel, tq=tq, tk=tk),
        grid=(batch, MLA_HEADS, nq),
        in_specs=[
            pl.BlockSpec((None, None, 2 * per, MLA_QK_PAD, tk), lambda b, h, i: (b, h, i, 0, 0)),
            pl.BlockSpec((seq, MLA_QK_PAD), lambda b, h, i: (b, h)),
            pl.BlockSpec((None, None, nk, d_ext, tk), lambda b, h, i: (b, h, 0, 0, 0)),
        ],
        out_specs=pl.BlockSpec((per * tq, MLA_V_DIM), lambda b, h, i: (b * nq + i, h)),
        out_shape=jax.ShapeDtypeStruct((m, MLA_HEADS * MLA_V_DIM), BF16),
        scratch_shapes=[pltpu.VMEM((per, 2, tk, tq), F32), pltpu.VMEM((per, 2, 1, tq), F32),
                        pltpu.VMEM((per, d_ext, tq), F32)],
        compiler_params=_params("parallel", "parallel", "arbitrary"),
        name="mla_attention",
    )(qt, k, vt)


def _merge_kernel(x_ref, asb_ref, amla_ref, gl_ref, bg_ref, wosb_ref, womla_ref, wout_ref,
                  gpost_ref, o_ref):
    d = x_ref.shape[1]
    for j in range(x_ref.shape[0] // MERGE_SUB_ROWS):
        r = slice(j * MERGE_SUB_ROWS, (j + 1) * MERGE_SUB_ROWS)
        y_sb = jnp.dot(asb_ref[r, :], wosb_ref[...], preferred_element_type=F32)
        y_mla = jnp.dot(amla_ref[r, :], womla_ref[...], preferred_element_type=F32)
        gates = jax.nn.sigmoid(gl_ref[r, :].astype(F32) + bg_ref[...])
        mixed_in = gates[:, :d] * y_sb + gates[:, d:] * y_mla
        mixed = jnp.dot(mixed_in.astype(BF16), wout_ref[...], preferred_element_type=F32)
        o_ref[r, :] = x_ref[r, :] + _rms(mixed, gpost_ref[...])


def merge(x, a_sb, a_mla, gate_logits, b_gate, w_o_sb, w_o_mla, w_out, g_post, *, tm):
    m, d = x.shape
    tm = min(tm, m)
    const = lambda i: (0, 0)
    resident = dict(pipeline_mode=pl.Buffered(1))
    return pl.pallas_call(
        _merge_kernel,
        grid=(m // tm,),
        in_specs=[
            pl.BlockSpec((tm, d), lambda i: (i, 0)),
            pl.BlockSpec((tm, a_sb.shape[1]), lambda i: (i, 0)),
            pl.BlockSpec((tm, a_mla.shape[1]), lambda i: (i, 0)),
            pl.BlockSpec((tm, 2 * d), lambda i: (i, 0)),
            pl.BlockSpec(b_gate.shape, const),
            pl.BlockSpec(w_o_sb.shape, const, **resident),
            pl.BlockSpec(w_o_mla.shape, const, **resident),
            pl.BlockSpec(w_out.shape, const, **resident),
            pl.BlockSpec(g_post.shape, const),
        ],
        out_specs=pl.BlockSpec((tm, d), lambda i: (i, 0)),
        out_shape=jax.ShapeDtypeStruct((m, d), F32),
        compiler_params=_params("parallel"),
        name="merge",
    )(x, a_sb, a_mla, gate_logits, b_gate, w_o_sb, w_o_mla, w_out, g_post)


def _ffn_kernel(h_ref, gpre_ref, wg_ref, wu_ref, wd_ref, gpost_ref, o_ref, hn_ref, acc_ref):
    f = pl.program_id(1)

    @pl.when(f == 0)
    def _():
        hn_ref[...] = _rms(h_ref[...], gpre_ref[...]).astype(BF16)
        acc_ref[...] = jnp.zeros_like(acc_ref)

    hn = hn_ref[...]
    g = jnp.dot(hn, wg_ref[...], preferred_element_type=F32)
    up = jnp.dot(hn, wu_ref[...], preferred_element_type=F32)
    act = (g * jax.nn.sigmoid(g)) * up
    acc_ref[...] += jnp.dot(act.astype(BF16), wd_ref[...], preferred_element_type=F32)

    @pl.when(f == pl.num_programs(1) - 1)
    def _():
        o_ref[...] = h_ref[...] + _rms(acc_ref[...], gpost_ref[...])


def ffn(h, g_pre, w_gate_up, w_down, g_post, *, tm, tf):
    m, d = h.shape
    d_ff = w_down.shape[0]
    tm = min(tm, m)
    nf = d_ff // tf
    const = lambda i, f: (0, 0)
    return pl.pallas_call(
        _ffn_kernel,
        grid=(m // tm, nf),
        in_specs=[
            pl.BlockSpec((tm, d), lambda i, f: (i, 0)),
            pl.BlockSpec(g_pre.shape, const),
            pl.BlockSpec((d, tf), lambda i, f: (0, f)),
            pl.BlockSpec((d, tf), lambda i, f: (0, nf + f)),
            pl.BlockSpec((tf, d), lambda i, f: (f, 0)),
            pl.BlockSpec(g_post.shape, const),
        ],
        out_specs=pl.BlockSpec((tm, d), lambda i, f: (i, 0)),
        out_shape=jax.ShapeDtypeStruct((m, d), F32),
        scratch_shapes=[pltpu.VMEM((tm, d), BF16), pltpu.VMEM((tm, d), F32)],
        compiler_params=_params("parallel", "arbitrary"),
        name="ffn",
    )(h, g_pre, w_gate_up, w_gate_up, w_down, g_post)


def _rope_table(positions):
    half = MLA_ROPE_DIM // 2
    inv_freq = ROPE_THETA ** (-jnp.arange(half, dtype=F32) / half)
    ang = positions.astype(F32)[:, None] * inv_freq
    pad = jnp.zeros((positions.shape[0], LANES - MLA_ROPE_DIM), F32)
    return jnp.concatenate([jnp.cos(ang), jnp.sin(ang), pad], axis=1)


def _layer(l, x2, pos, norm_pre_mix, w_in, b_gate, q_norm, w_uq, kv_norm, w_ukv, w_o_sb, w_o_mla,
           w_out, norm_post_mix, norm_pre_ffn, w_gate_up, w_down, norm_post_ffn, *, batch, seq):
    d = x2.shape[1]
    sb_w = SB_HEADS * SB_HEAD_DIM
    o_qlat = 3 * sb_w
    o_kvlat = o_qlat + MLA_Q_RANK
    o_kr = o_kvlat + MLA_KV_RANK
    o_gate = o_kr + MLA_ROPE_DIM
    row = lambda v: v[l][None, :].astype(F32)

    w_in_l = w_in[l]
    w_gate_qkv = jnp.concatenate([w_in_l[:, o_gate:], w_in_l[:, :o_qlat]], axis=1).astype(BF16)
    w_lat = jnp.concatenate(
        [w_in_l[:, o_qlat:o_gate], jnp.zeros((d, LANES - MLA_ROPE_DIM), w_in_l.dtype)], axis=1).astype(BF16)

    qk_dim = MLA_NOPE_DIM + MLA_ROPE_DIM
    wq = w_uq[l].reshape(MLA_Q_RANK, MLA_HEADS, qk_dim)
    wq = jnp.pad(wq, ((0, 0), (0, 0), (0, MLA_QK_PAD - qk_dim)))
    wq = wq.reshape(MLA_Q_RANK, MLA_HEADS * MLA_QK_PAD).astype(BF16)
    wkv = w_ukv[l].reshape(MLA_KV_RANK, MLA_HEADS, MLA_NOPE_DIM + MLA_V_DIM)
    wkv = jnp.concatenate([wkv[:, :, :MLA_NOPE_DIM].reshape(MLA_KV_RANK, -1),
                           wkv[:, :, MLA_NOPE_DIM:].reshape(MLA_KV_RANK, -1)], axis=1).astype(BF16)

    g_pre = row(norm_pre_mix)
    sb_q_scale = -(SB_HEAD_DIM ** -0.5) * LOG2E
    n_gate = w_in_l.shape[1] - o_gate
    col_scale = jnp.concatenate([jnp.ones((1, n_gate), F32), jnp.full((1, sb_w), sb_q_scale, F32),
                                 jnp.ones((1, 2 * sb_w), F32)], axis=1)
    gate_qkv = norm_matmul(x2, g_pre, w_gate_qkv, col_scale, tm=IN_PROJ_TM, tn=IN_PROJ_TN)

    q_m, k_m, vt_m = mla_prep(x2, g_pre, w_lat, row(q_norm), wq, row(kv_norm), wkv, _rope_table(pos),
                              batch=batch, tm=MLA_KEY_BLOCK, q_scale=(qk_dim ** -0.5) * LOG2E)

    a_sb = sb_attention(gate_qkv, batch=batch, seq=seq, tq=SB_QUERY_BLOCK, first_col=n_gate // SB_HEAD_DIM)
    a_mla = mla_attention(q_m, k_m, vt_m, batch=batch, seq=seq)

    h1 = merge(x2, a_sb, a_mla, gate_qkv, row(b_gate), w_o_sb[l].astype(BF16),
               w_o_mla[l].astype(BF16), w_out[l].astype(BF16), row(norm_post_mix), tm=MERGE_TM)
    return ffn(h1, row(norm_pre_ffn), w_gate_up[l].astype(BF16), w_down[l].astype(BF16),
               row(norm_post_ffn), tm=FFN_TM, tf=FFN_TF)


def kernel(x, positions, norm_pre_mix, w_in, b_gate, q_norm, w_uq, kv_norm, w_ukv, w_o_sb, w_o_mla,
           w_out, norm_post_mix, norm_pre_ffn, w_gate_up, w_down, norm_post_ffn):
    batch, seq, d = x.shape
    h = x.reshape(batch * seq, d)
    pos = positions.reshape(batch * seq)
    for l in range(w_in.shape[0]):
        h = _layer(l, h, pos, norm_pre_mix, w_in, b_gate, q_norm, w_uq, kv_norm, w_ukv, w_o_sb,
                   w_o_mla, w_out, norm_post_mix, norm_pre_ffn, w_gate_up, w_down, norm_post_ffn,
                   batch=batch, seq=seq)
    return h.reshape(batch, seq, d)
```

```python
import functools
import math

import jax
import jax.numpy as jnp
from jax import lax
from jax.experimental import pallas as pl
from jax.experimental.pallas import tpu as pltpu

F32 = jnp.float32
BF16 = jnp.bfloat16

CHUNK = 64
SB_HEADS = 8
SB_HEAD_DIM = 128
MLA_HEADS = 8
MLA_NOPE_DIM = 128
MLA_ROPE_DIM = 64
MLA_V_DIM = 128
MLA_Q_RANK = 768
MLA_KV_RANK = 512
ROPE_THETA = 10000.0
NORM_EPS = 1e-6
NEG_INF = -1e30
LOG2E = math.log2(math.e)

LANES = 128
MLA_QK_PAD = 2 * LANES
VMEM_LIMIT = 56 * 1024 * 1024
SB_BLOCK = 256
SB_DEAD_LOG2 = -160.0
BF16_ROWS = 16
MLA_KEY_BLOCK = 512
MLA_Q_PER_STEP = 4
MLA_PREP_TILES_PER_STEP = 2
IN_PROJ_TM, IN_PROJ_TN = 1024, 1792
SB_QUERY_BLOCK = 16 * SB_BLOCK
MERGE_TM, MERGE_SUB_ROWS = 512, 256
FFN_TM, FFN_TF = 512, 512


def _params(*semantics):
    return pltpu.CompilerParams(dimension_semantics=semantics, vmem_limit_bytes=VMEM_LIMIT)


def _rms(x, g):
    ms = jnp.mean(x * x, axis=-1, keepdims=True)
    return x * lax.rsqrt(ms + NORM_EPS) * g


def _norm_matmul_kernel(x_ref, g_ref, w_ref, cs_ref, o_ref, u_ref):
    @pl.when(pl.program_id(1) == 0)
    def _():
        u_ref[...] = _rms(x_ref[...], g_ref[...]).astype(BF16)

    acc = jnp.dot(u_ref[...], w_ref[...], preferred_element_type=F32)
    o_ref[...] = (acc * cs_ref[...]).astype(o_ref.dtype)


def norm_matmul(x, gain, w, col_scale, *, tm, tn):
    m, d = x.shape
    n = w.shape[1]
    tm = min(tm, m)
    return pl.pallas_call(
        _norm_matmul_kernel,
        grid=(m // tm, n // tn),
        in_specs=[
            pl.BlockSpec((tm, d), lambda i, j: (i, 0)),
            pl.BlockSpec((1, d), lambda i, j: (0, 0)),
            pl.BlockSpec((d, tn), lambda i, j: (0, j)),
            pl.BlockSpec((1, tn), lambda i, j: (0, j)),
        ],
        out_specs=pl.BlockSpec((tm, tn), lambda i, j: (i, j)),
        out_shape=jax.ShapeDtypeStruct((m, n), BF16),
        scratch_shapes=[pltpu.VMEM((tm, d), BF16)],
        compiler_params=_params("parallel", "arbitrary"),
        name="norm_matmul",
    )(x, gain, w, col_scale)


def _rope_coeffs(rope_ref):
    half = MLA_ROPE_DIM // 2
    t = rope_ref[...]
    lane = lax.broadcasted_iota(jnp.int32, t.shape, 1)
    first, second = lane < half, jnp.logical_and(lane >= half, lane < 2 * half)
    c = jnp.where(first, t, jnp.where(second, pltpu.roll(t, half, 1), 1.0))
    a = jnp.where(first, -pltpu.roll(t, LANES - half, 1), 0.0)
    b = jnp.where(second, t, 0.0)
    return c, a, b


def _rope128(r, coeffs):
    c, a, b = coeffs
    half = MLA_ROPE_DIM // 2
    return r * c + pltpu.roll(r, LANES - half, 1) * a + pltpu.roll(r, half, 1) * b


def _mla_prep_kernel(x_ref, g_ref, wlat_ref, qn_ref, wuq_ref, kvn_ref, wukv_ref, rope_ref,
                     qt_ref, k_ref, vt_ref, *, q_scale, tm):
    for j in range(x_ref.shape[0] // tm):
        r = slice(j * tm, (j + 1) * tm)
        _mla_prep_tile(x_ref.at[r], g_ref, wlat_ref, qn_ref, wuq_ref, kvn_ref, wukv_ref, rope_ref.at[r],
                       qt_ref.at[:, j], k_ref.at[r], vt_ref.at[:, j], q_scale=q_scale)


def _mla_prep_tile(x_ref, g_ref, wlat_ref, qn_ref, wuq_ref, kvn_ref, wukv_ref, rope_ref,
                   qt_ref, k_ref, vt_ref, *, q_scale):
    u = _rms(x_ref[...], g_ref[...]).astype(BF16)
    lat = jnp.dot(u, wlat_ref[...], preferred_element_type=F32)
    q_lat = lat[:, :MLA_Q_RANK]
    kv_lat = lat[:, MLA_Q_RANK:MLA_Q_RANK + MLA_KV_RANK]
    k_r = lat[:, MLA_Q_RANK + MLA_KV_RANK:]

    q = jnp.dot(_rms(q_lat, qn_ref[...]).astype(BF16), wuq_ref[...], preferred_element_type=F32)
    kv = jnp.dot(_rms(kv_lat, kvn_ref[...]).astype(BF16), wukv_ref[...], preferred_element_type=F32)
    coeffs = _rope_coeffs(rope_ref)
    k_rope = _rope128(k_r, coeffs).astype(BF16)
    ones = jnp.ones((BF16_ROWS, x_ref.shape[0]), BF16)

    kv_split = MLA_HEADS * MLA_NOPE_DIM
    for h in range(MLA_HEADS):
        lo = h * MLA_QK_PAD
        q_nope = q[:, lo:lo + LANES]
        q_rope = _rope128(q[:, lo + LANES:lo + 2 * LANES], coeffs)
        qt_ref[h, :LANES, :] = (q_nope * q_scale).T.astype(BF16)
        qt_ref[h, LANES:, :] = (q_rope * q_scale).T.astype(BF16)
        k_ref[:, lo:lo + LANES] = kv[:, h * LANES:(h + 1) * LANES].astype(BF16)
        k_ref[:, lo + LANES:lo + 2 * LANES] = k_rope
        v_h = kv[:, kv_split + h * MLA_V_DIM:kv_split + (h + 1) * MLA_V_DIM]
        vt_ref[h, :MLA_V_DIM, :] = v_h.T.astype(BF16)
        vt_ref[h, MLA_V_DIM:, :] = ones


def mla_prep(x, gain, w_lat, q_norm, w_uq, kv_norm, w_ukv, rope_tab, *, batch, tm, q_scale):
    m, d = x.shape
    n_tiles = m // batch // tm
    per = MLA_PREP_TILES_PER_STEP
    steps = n_tiles // per
    const = lambda i: (0, 0)
    once = dict(pipeline_mode=pl.Buffered(1))
    tile = lambda i: (i // steps, 0, i % steps, 0, 0)
    qk_w = MLA_HEADS * MLA_QK_PAD
    d_ext = MLA_V_DIM + BF16_ROWS
    return pl.pallas_call(
        functools.partial(_mla_prep_kernel, q_scale=q_scale, tm=tm),
        grid=(m // (per * tm),),
        in_specs=[
            pl.BlockSpec((per * tm, d), lambda i: (i, 0)),
            pl.BlockSpec(gain.shape, const),
            pl.BlockSpec(w_lat.shape, const, **once),
            pl.BlockSpec(q_norm.shape, const),
            pl.BlockSpec(w_uq.shape, const, **once),
            pl.BlockSpec(kv_norm.shape, const),
            pl.BlockSpec(w_ukv.shape, const, **once),
            pl.BlockSpec((per * tm, LANES), lambda i: (i, 0)),
        ],
        out_specs=[
            pl.BlockSpec((None, MLA_HEADS, per, MLA_QK_PAD, tm), tile),
            pl.BlockSpec((per * tm, qk_w), lambda i: (i, 0)),
            pl.BlockSpec((None, MLA_HEADS, per, d_ext, tm), tile),
        ],
        out_shape=[
            jax.ShapeDtypeStruct((batch, MLA_HEADS, n_tiles, MLA_QK_PAD, tm), BF16),
            jax.ShapeDtypeStruct((m, qk_w), BF16),
            jax.ShapeDtypeStruct((batch, MLA_HEADS, n_tiles, d_ext, tm), BF16),
        ],
        compiler_params=_params("parallel"),
        name="mla_prep",
    )(x, gain, w_lat, q_norm, w_uq, kv_norm, w_ukv, rope_tab)


def _qkt(q, k):
    return lax.dot_general(q, k, (((1,), (1,)), ((), ())), preferred_element_type=F32)


def _sb_attn_kernel(q_ref, k_ref, v_ref, o_ref, run_ref, acc_ref, *, tq):
    tb = SB_BLOCK
    i = pl.program_id(2)
    subs = range(tq // tb)
    row = lax.broadcasted_iota(jnp.int32, (tb, tb), 0)
    col = lax.broadcasted_iota(jnp.int32, (tb, tb), 1)
    before = row > col
    later = before.astype(BF16)

    def log2_keep(nz):
        return jnp.minimum(nz, 0.0) - jnp.log(1.0 + jnp.exp2(-jnp.abs(nz))) * LOG2E

    def suffix(keep):
        return jnp.dot(keep.astype(BF16), later, preferred_element_type=F32)

    q_first = [i * tq + g * tb for g in subs]
    starts = [pl.multiple_of(jnp.maximum(q_first[g] - tb, 0), tb) for g in subs]
    nz = [_qkt(q_ref[g * tb:(g + 1) * tb, :], k_ref[pl.ds(starts[g], 2 * tb), :]) for g in subs]
    nz_lo = [x[:, :tb] for x in nz]
    nz_hi = [x[:, tb:] for x in nz]
    delta = q_first[0] - starts[0]
    valid_lo = [(col - row) < delta] + [None for _ in subs[1:]]
    valid_hi = [(col - row) < delta - tb] + [before for _ in subs[1:]]
    keep_lo = [log2_keep(x) for x in nz_lo]
    keep_lo[0] = jnp.where(valid_lo[0], keep_lo[0], 0.0)
    keep_hi = [jnp.where(valid_hi[g], log2_keep(nz_hi[g]), 0.0) for g in subs]
    suf_hi = [suffix(x) for x in keep_hi]
    suf_lo = [suffix(x) for x in keep_lo]
    run_lo = [suf_hi[g][:, 0:1] + keep_hi[g][:, 0:1] for g in subs]
    w_hi = [jnp.where(valid_hi[g], jnp.exp2((keep_hi[g] - nz_hi[g]) + suf_hi[g]), 0.0) for g in subs]
    w_lo = [jnp.exp2((keep_lo[g] - nz_lo[g]) + suf_lo[g] + run_lo[g]) for g in subs]
    w_lo[0] = jnp.where(valid_lo[0], w_lo[0], 0.0)
    for g in subs:
        w = jnp.concatenate([w_lo[g], w_hi[g]], axis=1).astype(BF16)
        acc_ref[g * tb:(g + 1) * tb, :] = jnp.dot(w, v_ref[pl.ds(starts[g], 2 * tb), :],
                                                  preferred_element_type=F32)
        run_ref[g * tb:(g + 1) * tb, :] = run_lo[g] + suf_lo[g][:, 0:1] + keep_lo[g][:, 0:1]

    def walk_back(g):
        rows = pl.ds(g * tb, tb)

        def alive(c):
            return jnp.logical_and(c[0] >= 0, c[1] > SB_DEAD_LOG2)

        def body(c):
            start = pl.multiple_of(c[0] * tb, tb)
            nz = _qkt(q_ref[rows, :], k_ref[pl.ds(start, tb), :])
            keep = log2_keep(nz)
            suf = suffix(keep)
            run = run_ref[rows, :]
            w = jnp.exp2((keep - nz) + suf + run)
            acc_ref[rows, :] += jnp.dot(w.astype(BF16), v_ref[pl.ds(start, tb), :],
                                        preferred_element_type=F32)
            run = run + suf[:, 0:1] + keep[:, 0:1]
            run_ref[rows, :] = run
            return c[0] - 1, jnp.max(run)

        lax.while_loop(alive, body, (i * len(subs) + g - 2, jnp.max(run_ref[rows, :])))

    @pl.when(jnp.max(run_ref[...]) > SB_DEAD_LOG2)
    def _():
        for g in subs:
            walk_back(g)

    o_ref[...] = acc_ref[...].astype(o_ref.dtype)


def sb_attention(qkv, *, batch, seq, tq, first_col):
    m = qkv.shape[0]
    d = SB_HEAD_DIM
    tq = min(tq, seq)
    assert tq % SB_BLOCK == 0 and seq >= 2 * SB_BLOCK
    nq = seq // tq
    return pl.pallas_call(
        functools.partial(_sb_attn_kernel, tq=tq),
        grid=(batch, SB_HEADS, nq),
        in_specs=[
            pl.BlockSpec((tq, d), lambda b, h, i: (b * nq + i, first_col + h)),
            pl.BlockSpec((seq, d), lambda b, h, i: (b, first_col + SB_HEADS + h)),
            pl.BlockSpec((seq, d), lambda b, h, i: (b, first_col + 2 * SB_HEADS + h)),
        ],
        out_specs=pl.BlockSpec((tq, d), lambda b, h, i: (b * nq + i, h)),
        out_shape=jax.ShapeDtypeStruct((m, SB_HEADS * d), BF16),
        scratch_shapes=[pltpu.VMEM((tq, 1), F32), pltpu.VMEM((tq, d), F32)],
        compiler_params=_params("parallel", "parallel", "arbitrary"),
        name="sb_attention",
    )(qkv, qkv, qkv)


def _mla_attn_kernel(qt_ref, k_ref, vt_ref, o_ref, s_ref, smax_ref, acc_ref, *, tq, tk):
    for j in range(MLA_Q_PER_STEP):
        _mla_query_block(pl.program_id(2) * MLA_Q_PER_STEP + j, qt_ref.at[2 * j:2 * j + 2], k_ref, vt_ref,
                         o_ref.at[j * tq:(j + 1) * tq], s_ref.at[j], smax_ref.at[j], acc_ref.at[j],
                         tq=tq, tk=tk)


def _mla_query_block(i, qt_ref, k_ref, vt_ref, o_ref, s_ref, smax_ref, acc_ref, *, tq, tk):
    assert tq == 2 * tk

    def scores(blk, slot):
        k_blk = k_ref[pl.ds(pl.multiple_of(blk * tk, tk), tk), :]
        for half in range(2):
            s = jnp.dot(k_blk, qt_ref[half], preferred_element_type=F32)
            s_ref[slot, :, half * tk:(half + 1) * tk] = s
            smax_ref[slot, :, half * tk:(half + 1) * tk] = jnp.max(s, axis=0, keepdims=True)

    def update(blk, slot, m_old, mask):
        s = s_ref[slot]
        if mask is None:
            s_max = smax_ref[slot]
        else:
            s = jnp.where(mask, s, NEG_INF)
            s_max = jnp.max(s, axis=0, keepdims=True)
        m_new = jnp.maximum(m_old, s_max)
        p = jnp.exp2(s - m_new).astype(BF16)
        pv = jnp.dot(vt_ref[blk], p, preferred_element_type=F32)
        acc_ref[...] = jnp.exp2(m_old - m_new) * acc_ref[...] + pv
        return m_new

    acc_ref[...] = jnp.zeros(acc_ref.shape, F32)
    m = jnp.full((1, tq), NEG_INF, F32)
    scores(0, 0)

    def pair(t, m):
        scores(2 * t + 1, 1)
        m = update(2 * t, 0, m, None)
        scores(2 * t + 2, 0)
        return update(2 * t + 1, 1, m, None)

    m = lax.fori_loop(0, i // 2, lambda u, m: pair(2 * u + 1, pair(2 * u, m)), m)
    m = lax.cond(i % 2 == 1, lambda m: pair(i - 1, m), lambda m: m, m)
    k_chunk = lax.broadcasted_iota(jnp.int32, (tk, tq), 0) // CHUNK
    q_chunk = lax.broadcasted_iota(jnp.int32, (tk, tq), 1) // CHUNK
    allowed = k_chunk <= q_chunk
    blk = 2 * i + 1
    s = jnp.dot(k_ref[pl.ds(pl.multiple_of(blk * tk, tk), tk), :], qt_ref[1],
                preferred_element_type=F32)
    m = update(2 * i, 0, m, allowed)
    s = jnp.where(allowed[:, :tk], s, NEG_INF)
    m_old = m[:, tk:]
    m_new = jnp.maximum(m_old, jnp.max(s, axis=0, keepdims=True))
    pv = jnp.dot(vt_ref[blk], jnp.exp2(s - m_new).astype(BF16), preferred_element_type=F32)
    acc_ref[:, tk:] = jnp.exp2(m_old - m_new) * acc_ref[:, tk:] + pv
    out_t = acc_ref[:MLA_V_DIM, :] / acc_ref[MLA_V_DIM:MLA_V_DIM + 1, :]
    o_ref[...] = out_t.T.astype(o_ref.dtype)


def mla_attention(qt, k, vt, *, batch, seq):
    m = k.shape[0]
    _, _, nk, d_ext, tk = vt.shape
    tq = 2 * tk
    assert tk % CHUNK == 0 and nk * tk == seq
    per = MLA_Q_PER_STEP
    nq = seq // (per * tq)
    return pl.pallas_call(
        functools.partial(_mla_attn_kernel, tq=tq, tk=tk),
        grid=(batch, MLA_HEADS, nq),
        in_specs=[
            pl.BlockSpec((None, None, 2 * per, MLA_QK_PAD, tk), lambda b, h, i: (b, h, i, 0, 0)),
            pl.BlockSpec((seq, MLA_QK_PAD), lambda b, h, i: (b, h)),
            pl.BlockSpec((None, None, nk, d_ext, tk), lambda b, h, i: (b, h, 0, 0, 0)),
        ],
        out_specs=pl.BlockSpec((per * tq, MLA_V_DIM), lambda b, h, i: (b * nq + i, h)),
        out_shape=jax.ShapeDtypeStruct((m, MLA_HEADS * MLA_V_DIM), BF16),
        scratch_shapes=[pltpu.VMEM((per, 2, tk, tq), F32), pltpu.VMEM((per, 2, 1, tq), F32),
                        pltpu.VMEM((per, d_ext, tq), F32)],
        compiler_params=_params("parallel", "parallel", "arbitrary"),
        name="mla_attention",
    )(qt, k, vt)


def _merge_kernel(x_ref, asb_ref, amla_ref, gl_ref, bg_ref, wosb_ref, womla_ref, wout_ref,
                  gpost_ref, o_ref):
    d = x_ref.shape[1]
    for j in range(x_ref.shape[0] // MERGE_SUB_ROWS):
        r = slice(j * MERGE_SUB_ROWS, (j + 1) * MERGE_SUB_ROWS)
        y_sb = jnp.dot(asb_ref[r, :], wosb_ref[...], preferred_element_type=F32)
        y_mla = jnp.dot(amla_ref[r, :], womla_ref[...], preferred_element_type=F32)
        gates = jax.nn.sigmoid(gl_ref[r, :].astype(F32) + bg_ref[...])
        mixed_in = gates[:, :d] * y_sb + gates[:, d:] * y_mla
        mixed = jnp.dot(mixed_in.astype(BF16), wout_ref[...], preferred_element_type=F32)
        o_ref[r, :] = x_ref[r, :] + _rms(mixed, gpost_ref[...])


def merge(x, a_sb, a_mla, gate_logits, b_gate, w_o_sb, w_o_mla, w_out, g_post, *, tm):
    m, d = x.shape
    tm = min(tm, m)
    const = lambda i: (0, 0)
    resident = dict(pipeline_mode=pl.Buffered(1))
    return pl.pallas_call(
        _merge_kernel,
        grid=(m // tm,),
        in_specs=[
            pl.BlockSpec((tm, d), lambda i: (i, 0)),
            pl.BlockSpec((tm, a_sb.shape[1]), lambda i: (i, 0)),
            pl.BlockSpec((tm, a_mla.shape[1]), lambda i: (i, 0)),
            pl.BlockSpec((tm, 2 * d), lambda i: (i, 0)),
            pl.BlockSpec(b_gate.shape, const),
            pl.BlockSpec(w_o_sb.shape, const, **resident),
            pl.BlockSpec(w_o_mla.shape, const, **resident),
            pl.BlockSpec(w_out.shape, const, **resident),
            pl.BlockSpec(g_post.shape, const),
        ],
        out_specs=pl.BlockSpec((tm, d), lambda i: (i, 0)),
        out_shape=jax.ShapeDtypeStruct((m, d), F32),
        compiler_params=_params("parallel"),
        name="merge",
    )(x, a_sb, a_mla, gate_logits, b_gate, w_o_sb, w_o_mla, w_out, g_post)


def _ffn_kernel(h_ref, gpre_ref, wg_ref, wu_ref, wd_ref, gpost_ref, o_ref, hn_ref, acc_ref):
    f = pl.program_id(1)

    @pl.when(f == 0)
    def _():
        hn_ref[...] = _rms(h_ref[...], gpre_ref[...]).astype(BF16)
        acc_ref[...] = jnp.zeros_like(acc_ref)

    hn = hn_ref[...]
    g = jnp.dot(hn, wg_ref[...], preferred_element_type=F32)
    up = jnp.dot(hn, wu_ref[...], preferred_element_type=F32)
    act = (g * jax.nn.sigmoid(g)) * up
    acc_ref[...] += jnp.dot(act.astype(BF16), wd_ref[...], preferred_element_type=F32)

    @pl.when(f == pl.num_programs(1) - 1)
    def _():
        o_ref[...] = h_ref[...] + _rms(acc_ref[...], gpost_ref[...])


def ffn(h, g_pre, w_gate_up, w_down, g_post, *, tm, tf):
    m, d = h.shape
    d_ff = w_down.shape[0]
    tm = min(tm, m)
    nf = d_ff // tf
    const = lambda i, f: (0, 0)
    return pl.pallas_call(
        _ffn_kernel,
        grid=(m // tm, nf),
        in_specs=[
            pl.BlockSpec((tm, d), lambda i, f: (i, 0)),
            pl.BlockSpec(g_pre.shape, const),
            pl.BlockSpec((d, tf), lambda i, f: (0, f)),
            pl.BlockSpec((d, tf), lambda i, f: (0, nf + f)),
            pl.BlockSpec((tf, d), lambda i, f: (f, 0)),
            pl.BlockSpec(g_post.shape, const),
        ],
        out_specs=pl.BlockSpec((tm, d), lambda i, f: (i, 0)),
        out_shape=jax.ShapeDtypeStruct((m, d), F32),
        scratch_shapes=[pltpu.VMEM((tm, d), BF16), pltpu.VMEM((tm, d), F32)],
        compiler_params=_params("parallel", "arbitrary"),
        name="ffn",
    )(h, g_pre, w_gate_up, w_gate_up, w_down, g_post)


def _rope_table(positions):
    half = MLA_ROPE_DIM // 2
    inv_freq = ROPE_THETA ** (-jnp.arange(half, dtype=F32) / half)
    ang = positions.astype(F32)[:, None] * inv_freq
    pad = jnp.zeros((positions.shape[0], LANES - MLA_ROPE_DIM), F32)
    return jnp.concatenate([jnp.cos(ang), jnp.sin(ang), pad], axis=1)


def _layer(l, x2, pos, norm_pre_mix, w_in, b_gate, q_norm, w_uq, kv_norm, w_ukv, w_o_sb, w_o_mla,
           w_out, norm_post_mix, norm_pre_ffn, w_gate_up, w_down, norm_post_ffn, *, batch, seq):
    d = x2.shape[1]
    sb_w = SB_HEADS * SB_HEAD_DIM
    o_qlat = 3 * sb_w
    o_kvlat = o_qlat + MLA_Q_RANK
    o_kr = o_kvlat + MLA_KV_RANK
    o_gate = o_kr + MLA_ROPE_DIM
    row = lambda v: v[l][None, :].astype(F32)

    w_in_l = w_in[l]
    w_gate_qkv = jnp.concatenate([w_in_l[:, o_gate:], w_in_l[:, :o_qlat]], axis=1).astype(BF16)
    w_lat = jnp.concatenate(
        [w_in_l[:, o_qlat:o_gate], jnp.zeros((d, LANES - MLA_ROPE_DIM), w_in_l.dtype)], axis=1).astype(BF16)

    qk_dim = MLA_NOPE_DIM + MLA_ROPE_DIM
    wq = w_uq[l].reshape(MLA_Q_RANK, MLA_HEADS, qk_dim)
    wq = jnp.pad(wq, ((0, 0), (0, 0), (0, MLA_QK_PAD - qk_dim)))
    wq = wq.reshape(MLA_Q_RANK, MLA_HEADS * MLA_QK_PAD).astype(BF16)
    wkv = w_ukv[l].reshape(MLA_KV_RANK, MLA_HEADS, MLA_NOPE_DIM + MLA_V_DIM)
    wkv = jnp.concatenate([wkv[:, :, :MLA_NOPE_DIM].reshape(MLA_KV_RANK, -1),
                           wkv[:, :, MLA_NOPE_DIM:].reshape(MLA_KV_RANK, -1)], axis=1).astype(BF16)

    g_pre = row(norm_pre_mix)
    sb_q_scale = -(SB_HEAD_DIM ** -0.5) * LOG2E
    n_gate = w_in_l.shape[1] - o_gate
    col_scale = jnp.concatenate([jnp.ones((1, n_gate), F32), jnp.full((1, sb_w), sb_q_scale, F32),
                                 jnp.ones((1, 2 * sb_w), F32)], axis=1)
    gate_qkv = norm_matmul(x2, g_pre, w_gate_qkv, col_scale, tm=IN_PROJ_TM, tn=IN_PROJ_TN)

    q_m, k_m, vt_m = mla_prep(x2, g_pre, w_lat, row(q_norm), wq, row(kv_norm), wkv, _rope_table(pos),
                              batch=batch, tm=MLA_KEY_BLOCK, q_scale=(qk_dim ** -0.5) * LOG2E)

    a_sb = sb_attention(gate_qkv, batch=batch, seq=seq, tq=SB_QUERY_BLOCK, first_col=n_gate // SB_HEAD_DIM)
    a_mla = mla_attention(q_m, k_m, vt_m, batch=batch, seq=seq)

    h1 = merge(x2, a_sb, a_mla, gate_qkv, row(b_gate), w_o_sb[l].astype(BF16),
               w_o_mla[l].astype(BF16), w_out[l].astype(BF16), row(norm_post_mix), tm=MERGE_TM)
    return ffn(h1, row(norm_pre_ffn), w_gate_up[l].astype(BF16), w_down[l].astype(BF16),
               row(norm_post_ffn), tm=FFN_TM, tf=FFN_TF)


def kernel(x, positions, norm_pre_mix, w_in, b_gate, q_norm, w_uq, kv_norm, w_ukv, w_o_sb, w_o_mla,
           w_out, norm_post_mix, norm_pre_ffn, w_gate_up, w_down, norm_post_ffn):
    batch, seq, d = x.shape
    h = x.reshape(batch * seq, d)
    pos = positions.reshape(batch * seq)
    for l in range(w_in.shape[0]):
        h = _layer(l, h, pos, norm_pre_mix, w_in, b_gate, q_norm, w_uq, kv_norm, w_ukv, w_o_sb,
                   w_o_mla, w_out, norm_post_mix, norm_pre_ffn, w_gate_up, w_down, norm_post_ffn,
                   batch=batch, seq=seq)
    return h.reshape(batch, seq, d)
```
